```python
import jax
import jax.numpy as jnp
from jax import lax
import numpy as np

D_MODEL = 4096
BATCH = 1
SEQ = 8192
DEPTH = 1
DEC_BATCH = 32
DEC_SEQ = 8
PAST_LEN = 8192
PAGE_SIZE = 128

RET_HEADS = 8
RET_DK = D_MODEL // (2 * RET_HEADS)
RET_DV = 2 * RET_DK
RET_CHUNK = 128
ROT_BASE = 10000.0
ATT_HEADS = 16
ATT_DH = D_MODEL // (2 * ATT_HEADS)
IDX_HEADS = 16
IDX_DH = 128
TOPK_MAX = 256
Q_BLOCK = 128
N_EXPERTS = 256
MOE_TOPK = 8
N_GROUPS = 8
TOPK_GROUPS = 4
D_EXPERT = D_MODEL // 8
ROUTED_SCALE = 2.5
LN_EPS = 1e-5
DN_ALPHA = (2.0 * DEPTH) ** 0.25
DN_BETA = (8.0 * DEPTH) ** -0.25
RET_QK = RET_HEADS * RET_DK
RET_V = RET_HEADS * RET_DV
ATT_W = ATT_HEADS * ATT_DH
IDX_Q = IDX_HEADS * IDX_DH
IN_WIDTHS = (RET_QK, RET_QK, RET_V, RET_V, ATT_W, ATT_W, ATT_W, IDX_Q, IDX_DH, IDX_HEADS, D_MODEL, D_MODEL)
D_IN = sum(IN_WIDTHS)

kernel_name = 'hybrid_retention_dsa_moe_decode_step'


def layer_norm(x):
    xf = x.astype(jnp.float32)
    mu = jnp.mean(xf, axis=-1, keepdims=True)
    var = jnp.mean(jnp.square(xf - mu), axis=-1, keepdims=True)
    return ((xf - mu) * lax.rsqrt(var + LN_EPS)).astype(x.dtype)


def post_norm(x, y, gamma, beta):
    return layer_norm(DN_ALPHA * x + y) * gamma + beta


def modulation(c, w_cond, b_cond):
    m = jnp.dot(jax.nn.silu(c), w_cond) + b_cond
    return jnp.split(m[:, None, :], 6, axis=-1)


def rotate(x, pos):
    half = x.shape[-1] // 2
    inv = ROT_BASE ** (-jnp.arange(half, dtype=jnp.float32) / half)
    ang = pos.astype(jnp.float32)[:, None] * inv[None, :]
    cos = jnp.cos(ang)[None, :, None, :]
    sin = jnp.sin(ang)[None, :, None, :]
    xf = x.astype(jnp.float32)
    x1, x2 = xf[..., :half], xf[..., half:]
    return jnp.concatenate([x1 * cos - x2 * sin, x2 * cos + x1 * sin], axis=-1).astype(x.dtype)


def mixer_projections(h, pos, w_in):
    b, t, _ = h.shape
    cuts = np.cumsum(IN_WIDTHS)[:-1].tolist()
    rq, rk, rv, rg, aq, ak, av, iq, ik, iw, ga, gb = jnp.split(jnp.dot(h, w_in), cuts, axis=-1)
    rq = rotate(rq.reshape(b, t, RET_HEADS, RET_DK), pos)
    rk = rotate(rk.reshape(b, t, RET_HEADS, RET_DK), pos) * (RET_DK ** -0.5)
    rv = rv.reshape(b, t, RET_HEADS, RET_DV)
    aq = aq.reshape(b, t, ATT_HEADS, ATT_DH)
    ak = ak.reshape(b, t, ATT_HEADS, ATT_DH)
    av = av.reshape(b, t, ATT_HEADS, ATT_DH)
    iq = iq.reshape(b, t, IDX_HEADS, IDX_DH) * (IDX_DH ** -0.5)
    iw = iw * (IDX_HEADS ** -0.5)
    return rq, rk, rv, rg, aq, ak, av, iq, ik, iw, ga, gb


def retention_chunkwise(q, k, v, s0, chunk):
    b, t, nh, dk = q.shape
    dv = v.shape[-1]
    n = t // chunk
    lg = jnp.log1p(-jnp.exp2(-5.0 - jnp.arange(nh, dtype=jnp.float32)))
    idx = jnp.arange(chunk, dtype=jnp.float32)
    diff = idx[:, None] - idx[None, :]
    intra = jnp.where(diff >= 0, jnp.exp(lg[:, None, None] * jnp.maximum(diff, 0.0)), 0.0)
    q_dec = jnp.exp(lg[:, None] * (idx[None, :] + 1.0))
    k_dec = jnp.exp(lg[:, None] * (chunk - 1.0 - idx[None, :]))
    c_dec = jnp.exp(lg * chunk)

    def to_chunks(a):
        return a.reshape(b, n, chunk, nh, a.shape[-1]).transpose(1, 0, 3, 2, 4)

    def step(s, inp):
        qi, ki, vi = (a.astype(jnp.float32) for a in inp)
        att = jnp.einsum('bhid,bhjd->bhij', qi, ki) * intra
        o = jnp.einsum('bhij,bhjv->bhiv', att, vi) + jnp.einsum('bhid,bhdv->bhiv', qi * q_dec[None, :, :, None], s)
        s_new = s * c_dec[None, :, None, None] + jnp.einsum('bhjd,bhjv->bhdv', ki * k_dec[None, :, :, None], vi)
        return s_new, o

    s_fin, oc = lax.scan(step, s0, (to_chunks(q), to_chunks(k), to_chunks(v)))
    o = oc.transpose(1, 0, 3, 2, 4).reshape(b, t, nh, dv)
    return o, s_fin


def sparse_attention_core(q, iq, iw, ik, qpos, n_sel, gather_kv):
    n_keys = ik.shape[1]
    rel = jax.nn.relu(jnp.einsum('bqhd,bsd->bqhs', iq, ik).astype(jnp.float32))
    score = jnp.einsum('bqhs,bqh->bqs', rel, iw.astype(jnp.float32))
    causal = jnp.arange(n_keys)[None, :] <= qpos[:, None]
    score = jnp.where(causal[None], score, -jnp.inf)
    _, sel = lax.top_k(score, n_sel)
    valid = sel <= qpos[None, :, None]
    kg, vg = gather_kv(sel)
    logits = jnp.einsum('bqhd,bqkhd->bqhk', q, kg).astype(jnp.float32) * (ATT_DH ** -0.5)
    logits = jnp.where(valid[:, :, None, :], logits, -jnp.inf)
    p = jax.nn.softmax(logits, axis=-1)
    return jnp.einsum('bqhk,bqkhd->bqhd', p.astype(vg.dtype), vg)


def prompt_sparse_attention(aq, ak, av, iq, ik, iw):
    b, t, nh, dh = aq.shape
    n_sel = min(TOPK_MAX, t // 4)
    nb = t // Q_BLOCK
    bidx = jnp.arange(b)[:, None, None]

    def gather(sel):
        return ak[bidx, sel], av[bidx, sel]

    def blocks(a):
        return a.reshape(b, nb, Q_BLOCK, *a.shape[2:]).swapaxes(0, 1)

    def one_block(args):
        q_b, iq_b, iw_b, pos_b = args
        return sparse_attention_core(q_b, iq_b, iw_b, ik, pos_b, n_sel, gather)

    pos = jnp.arange(t, dtype=jnp.int32).reshape(nb, Q_BLOCK)
    out = lax.map(one_block, (blocks(aq), blocks(iq), blocks(iw), pos))
    return out.swapaxes(0, 1).reshape(b, t, nh, dh)


def make_sample_attention(cache_k, cache_v, cache_idx_k, page_table):
    def attend(aq, ak, av, iq, ik, iw):
        b, t = aq.shape[:2]
        n_sel = min(TOPK_MAX, (PAST_LEN + t) // 4)
        past_ik = cache_idx_k[page_table].reshape(b, PAST_LEN, IDX_DH).astype(ik.dtype)
        ik_all = jnp.concatenate([past_ik, ik], axis=1)
        bidx = jnp.arange(b)[:, None, None]

        def gather(sel):
            is_new = (sel >= PAST_LEN)[..., None, None]
            pc = jnp.minimum(sel, PAST_LEN - 1)
            phys = page_table[bidx, pc // PAGE_SIZE]
            off = pc % PAGE_SIZE
            ni = jnp.clip(sel - PAST_LEN, 0, t - 1)
            kg = jnp.where(is_new, ak[bidx, ni], cache_k[phys, off].astype(ak.dtype))
            vg = jnp.where(is_new, av[bidx, ni], cache_v[phys, off].astype(av.dtype))
            return kg, vg

        qpos = PAST_LEN + jnp.arange(t, dtype=jnp.int32)
        return sparse_attention_core(aq, iq, iw, ik_all, qpos, n_sel, gather)
    return attend


def mixer_merge(ret_o, rg, att_o, ga, gb, w_ret_o, w_att_o, w_out):
    b, t = rg.shape[:2]
    rn = layer_norm(ret_o).reshape(b, t, RET_V).astype(rg.dtype)
    branch_a = jnp.dot(jax.nn.silu(rg) * rn, w_ret_o)
    branch_b = jnp.dot(att_o.reshape(b, t, ATT_W), w_att_o)
    return jnp.dot(jax.nn.sigmoid(ga) * branch_a + jax.nn.sigmoid(gb) * branch_b, w_out)


def route(h, w_router, b_router):
    m = h.shape[0]
    s = jax.nn.sigmoid(jnp.dot(h, w_router).astype(jnp.float32))
    sb = s + b_router.astype(jnp.float32)
    per = N_EXPERTS // N_GROUPS
    grp_score = lax.top_k(sb.reshape(m, N_GROUPS, per), 2)[0].sum(-1)
    _, grp_sel = lax.top_k(grp_score, TOPK_GROUPS)
    grp_mask = jnp.sum(jax.nn.one_hot(grp_sel, N_GROUPS, dtype=jnp.float32), axis=1) > 0
    sb = jnp.where(jnp.repeat(grp_mask, per, axis=1), sb, -jnp.inf)
    _, top_i = lax.top_k(sb, MOE_TOPK)
    top_s = jnp.take_along_axis(s, top_i, axis=1)
    top_w = top_s / jnp.sum(top_s, axis=-1, keepdims=True) * ROUTED_SCALE
    return top_i.astype(jnp.int32), top_w


def routed_experts(h, top_i, top_w, w_gate, w_up, w_down):
    m, d = h.shape
    n_assign = m * MOE_TOPK
    bm = max(8, min(128, n_assign // N_EXPERTS))
    n_blocks = -(-n_assign // bm) + N_EXPERTS
    flat_e = top_i.reshape(-1)
    order = jnp.argsort(flat_e)
    se = flat_e[order]
    s_tok = (order // MOE_TOPK).astype(jnp.int32)
    s_w = top_w.reshape(-1)[order]
    counts = jnp.bincount(flat_e, length=N_EXPERTS)
    padded = (counts + bm - 1) // bm * bm
    pad_end = jnp.cumsum(padded)
    dest = (pad_end - padded)[se] + jnp.arange(n_assign) - (jnp.cumsum(counts) - counts)[se]
    row_tok = jnp.full((n_blocks * bm,), m, jnp.int32).at[dest].set(s_tok)
    row_w = jnp.zeros((n_blocks * bm,), jnp.float32).at[dest].set(s_w)
    block_e = jnp.minimum(jnp.searchsorted(pad_end, jnp.arange(n_blocks) * bm, side='right'), N_EXPERTS - 1)
    h_pad = jnp.concatenate([h, jnp.zeros((1, d), h.dtype)], axis=0)

    def step(acc, blk):
        e, toks, wts = blk
        xb = h_pad[toks]
        a = jax.nn.silu(jnp.dot(xb, w_gate[e])) * jnp.dot(xb, w_up[e])
        yb = jnp.dot(a, w_down[e]).astype(jnp.float32) * wts[:, None]
        return acc.at[toks].add(yb), None

    acc, _ = lax.scan(step, jnp.zeros((m + 1, d), jnp.float32),
                      (block_e, row_tok.reshape(n_blocks, bm), row_w.reshape(n_blocks, bm)))
    return acc[:m]


def moe_ffn(h, w_router, b_router, w_exp_gate, w_exp_up, w_exp_down, w_sh_gate, w_sh_up, w_sh_down):
    b, t, d = h.shape
    hf = h.reshape(b * t, d)
    top_i, top_w = route(hf, w_router, b_router)
    routed = routed_experts(hf, top_i, top_w, w_exp_gate, w_exp_up, w_exp_down)
    shared = jnp.dot(jax.nn.silu(jnp.dot(hf, w_sh_gate)) * jnp.dot(hf, w_sh_up), w_sh_down)
    return (routed.astype(h.dtype) + shared).reshape(b, t, d)


def decoder_layer(x, c, pos, ret_s0, ret_chunk, attend, w_cond, b_cond, w_in, w_ret_o, w_att_o, w_out,
                  ln_mix_g, ln_mix_b, w_router, b_router, w_exp_gate, w_exp_up, w_exp_down,
                  w_sh_gate, w_sh_up, w_sh_down, ln_ffn_g, ln_ffn_b):
    shift_m, scale_m, gate_m, shift_f, scale_f, gate_f = modulation(c, w_cond, b_cond)
    h = layer_norm(x) * (1.0 + scale_m) + shift_m
    rq, rk, rv, rg, aq, ak, av, iq, ik, iw, ga, gb = mixer_projections(h, pos, w_in)
    ret_o, ret_state = retention_chunkwise(rq, rk, rv, ret_s0, ret_chunk)
    att_o = attend(aq, ak, av, iq, ik, iw)
    mix = mixer_merge(ret_o, rg, att_o, ga, gb, w_ret_o, w_att_o, w_out)
    x = post_norm(x, gate_m * mix, ln_mix_g, ln_mix_b)
    h = layer_norm(x) * (1.0 + scale_f) + shift_f
    ffn = moe_ffn(h, w_router, b_router, w_exp_gate, w_exp_up, w_exp_down, w_sh_gate, w_sh_up, w_sh_down)
    x = post_norm(x, gate_f * ffn, ln_ffn_g, ln_ffn_b)
    return x, ak, av, ik, ret_state


def setup_inputs(seed: int = 0) -> dict:
    key = jax.random.key(seed)
    ks = jax.random.split(key, 32)
    f32 = jnp.float32
    n_pages = PAST_LEN // PAGE_SIZE
    n_pool = (DEC_BATCH * n_pages * 5) // 4

    def nrm(k, shape, scale):
        return jax.random.normal(k, shape, f32) * scale

    page_table = jax.random.permutation(ks[0], n_pool)[: DEC_BATCH * n_pages].reshape(DEC_BATCH, n_pages).astype(jnp.int32)
    return {
        'x_prompt': nrm(ks[1], (BATCH, SEQ, D_MODEL), 1.0),
        'x_sample': nrm(ks[2], (DEC_BATCH, DEC_SEQ, D_MODEL), 1.0),
        'c_prompt': nrm(ks[3], (BATCH, D_MODEL), 1.0),
        'c_sample': nrm(ks[4], (DEC_BATCH, D_MODEL), 1.0),
        'cache_k': nrm(ks[5], (n_pool, PAGE_SIZE, ATT_HEADS, ATT_DH), 1.0),
        'cache_v': nrm(ks[6], (n_pool, PAGE_SIZE, ATT_HEADS, ATT_DH), 1.0),
        'cache_idx_k': nrm(ks[7], (n_pool, PAGE_SIZE, IDX_DH), 1.0),
        'state_ret': nrm(ks[8], (DEC_BATCH, RET_HEADS, RET_DK, RET_DV), 0.25),
        'page_table': page_table,
        'w_cond': nrm(ks[9], (D_MODEL, 6 * D_MODEL), 0.5 * D_MODEL ** -0.5),
        'b_cond': nrm(ks[10], (6 * D_MODEL,), 0.01),
        'w_in': nrm(ks[11], (D_MODEL, D_IN), D_MODEL ** -0.5),
        'w_ret_o': nrm(ks[12], (RET_V, D_MODEL), RET_V ** -0.5),
        'w_att_o': nrm(ks[13], (ATT_W, D_MODEL), ATT_W ** -0.5),
        'w_out': nrm(ks[14], (D_MODEL, D_MODEL), DN_BETA * D_MODEL ** -0.5),
        'ln_mix_g': 1.0 + nrm(ks[15], (D_MODEL,), 0.01),
        'ln_mix_b': nrm(ks[16], (D_MODEL,), 0.01),
        'w_router': nrm(ks[17], (D_MODEL, N_EXPERTS), D_MODEL ** -0.5),
        'b_router': nrm(ks[18], (N_EXPERTS,), 0.01),
        'w_exp_gate': nrm(ks[19], (N_EXPERTS, D_MODEL, D_EXPERT), D_MODEL ** -0.5),
        'w_exp_up': nrm(ks[20], (N_EXPERTS, D_MODEL, D_EXPERT), D_MODEL ** -0.5),
        'w_exp_down': nrm(ks[21], (N_EXPERTS, D_EXPERT, D_MODEL), DN_BETA * D_EXPERT ** -0.5),
        'w_sh_gate': nrm(ks[22], (D_MODEL, D_EXPERT), D_MODEL ** -0.5),
        'w_sh_up': nrm(ks[23], (D_MODEL, D_EXPERT), D_MODEL ** -0.5),
        'w_sh_down': nrm(ks[24], (D_EXPERT, D_MODEL), DN_BETA * D_EXPERT ** -0.5),
        'ln_ffn_g': 1.0 + nrm(ks[25], (D_MODEL,), 0.01),
        'ln_ffn_b': nrm(ks[26], (D_MODEL,), 0.01),
    }


def reference(x_prompt, x_sample, c_prompt, c_sample, cache_k, cache_v, cache_idx_k, state_ret, page_table,
              w_cond, b_cond, w_in, w_ret_o, w_att_o, w_out, ln_mix_g, ln_mix_b,
              w_router, b_router, w_exp_gate, w_exp_up, w_exp_down, w_sh_gate, w_sh_up, w_sh_down,
              ln_ffn_g, ln_ffn_b):
    weights = (w_cond, b_cond, w_in, w_ret_o, w_att_o, w_out, ln_mix_g, ln_mix_b,
               w_router, b_router, w_exp_gate, w_exp_up, w_exp_down, w_sh_gate, w_sh_up, w_sh_down,
               ln_ffn_g, ln_ffn_b)
    b_p, t_p, _ = x_prompt.shape
    y_prompt, k_p, v_p, ik_p, s_p = decoder_layer(
        x_prompt, c_prompt, jnp.arange(t_p, dtype=jnp.int32),
        jnp.zeros((b_p, RET_HEADS, RET_DK, RET_DV), jnp.float32), RET_CHUNK,
        prompt_sparse_attention, *weights)
    t_s = x_sample.shape[1]
    y_sample, k_s, v_s, ik_s, s_s = decoder_layer(
        x_sample, c_sample, PAST_LEN + jnp.arange(t_s, dtype=jnp.int32),
        state_ret.astype(jnp.float32), t_s,
        make_sample_attention(cache_k, cache_v, cache_idx_k, page_table), *weights)
    return (y_prompt, y_sample, k_p, v_p, ik_p, s_p, k_s, v_s, ik_s, s_s)
```

```python
import functools
import math

import jax
import jax.numpy as jnp
import numpy as np
from jax import lax
from jax.experimental import pallas as pl
from jax.experimental.pallas import tpu as pltpu

ROT_BASE = 10000.0
TOPK_MAX = 256
Q_BLOCK = 128
MOE_TOPK = 8
N_GROUPS = 8
TOPK_GROUPS = 4
ROUTED_SCALE = 2.5
LN_EPS = 1e-5
DEPTH = 1
DN_ALPHA = (2.0 * DEPTH) ** 0.25

LANE = 128
SUBLANE = 8
VMEM_BYTES_V7X = 64 * 2 ** 20

NEG = -1e30
INT_MIN = -2 ** 31
BF16 = jnp.bfloat16
F32 = jnp.float32

_NT = (((1,), (1,)), ((), ()))
_TN = (((0,), (0,)), ((), ()))


def _cparams(sem, vmem_mb):
    assert vmem_mb * 2 ** 20 < VMEM_BYTES_V7X
    return pltpu.CompilerParams(dimension_semantics=sem, vmem_limit_bytes=vmem_mb * 2 ** 20)


def _pick(n, cands):
    for c in cands:
        if c <= n and n % c == 0:
            return c
    return n


def _silu(x):
    return x * jax.nn.sigmoid(x)


def _layer_norm(x):
    mu = jnp.mean(x, axis=-1, keepdims=True)
    xc = x - mu
    var = jnp.mean(xc * xc, axis=-1, keepdims=True)
    return xc * lax.rsqrt(var + LN_EPS)


def _cond_kernel(c_ref, w_ref, b_ref, o_ref):
    a = _silu(c_ref[...]).astype(BF16)
    o_ref[...] = jnp.dot(a, w_ref[...].astype(BF16), preferred_element_type=F32) + b_ref[...]


def _modulation(c, w_cond, b_cond):
    nb, d = c.shape
    n = w_cond.shape[1]
    mp = -(-nb // SUBLANE) * SUBLANE
    cp = jnp.pad(c, ((0, mp - nb), (0, 0)))
    tn = _pick(n, (512, 256, 128))
    out = pl.pallas_call(
        _cond_kernel,
        grid=(n // tn,),
        in_specs=[pl.BlockSpec((mp, d), lambda j: (0, 0)),
                  pl.BlockSpec((d, tn), lambda j: (0, j)),
                  pl.BlockSpec((1, tn), lambda j: (0, j))],
        out_specs=pl.BlockSpec((mp, tn), lambda j: (0, j)),
        out_shape=jax.ShapeDtypeStruct((mp, n), F32),
        compiler_params=_cparams(("arbitrary",), 40),
    )(cp, w_cond, b_cond.reshape(1, n))
    return out[:nb]


def _ln_mod_kernel(x_ref, sc_ref, sh_ref, o_ref):
    y = _layer_norm(x_ref[0])
    o_ref[0] = (y * (1.0 + sc_ref[0]) + sh_ref[0]).astype(o_ref.dtype)


def _ln_mod(x, scale, shift, out_dtype):
    b, t, d = x.shape
    tt = _pick(t, (256, 128, 64, 32, 16, 8))
    mod = pl.BlockSpec((1, 1, d), lambda i, j: (i, 0, 0))
    row = pl.BlockSpec((1, tt, d), lambda i, j: (i, j, 0))
    return pl.pallas_call(
        _ln_mod_kernel,
        grid=(b, t // tt),
        in_specs=[row, mod, mod],
        out_specs=row,
        out_shape=jax.ShapeDtypeStruct((b, t, d), out_dtype),
        compiler_params=_cparams(("arbitrary", "arbitrary"), 40),
    )(x, scale, shift)


def _post_ln_kernel(x_ref, y_ref, gate_ref, gam_ref, bet_ref, sc_ref, sh_ref, x1_ref, h_ref):
    z = DN_ALPHA * x_ref[0] + gate_ref[0] * y_ref[0]
    x1 = _layer_norm(z) * gam_ref[...] + bet_ref[...]
    x1_ref[0] = x1
    h_ref[0] = (_layer_norm(x1) * (1.0 + sc_ref[0]) + sh_ref[0]).astype(h_ref.dtype)


def _post_ln(x, y, gate, gamma, beta, scale, shift):
    b, t, d = x.shape
    tt = _pick(t, (256, 128, 64, 32, 16, 8))
    mod = pl.BlockSpec((1, 1, d), lambda i, j: (i, 0, 0))
    row = pl.BlockSpec((1, tt, d), lambda i, j: (i, j, 0))
    vec = pl.BlockSpec((1, d), lambda i, j: (0, 0))
    return pl.pallas_call(
        _post_ln_kernel,
        grid=(b, t // tt),
        in_specs=[row, row, mod, vec, vec, mod, mod],
        out_specs=[row, row],
        out_shape=[jax.ShapeDtypeStruct((b, t, d), F32), jax.ShapeDtypeStruct((b, t, d), F32)],
        compiler_params=_cparams(("arbitrary", "arbitrary"), 48),
    )(x, y, gate, gamma.reshape(1, d), beta.reshape(1, d), scale, shift)


def _proj_kernel(*refs, n_extra, n_out, epilogue):
    a_ref, w_ref = refs[0], refs[1]
    extra = refs[2:2 + n_extra]
    outs = refs[2 + n_extra:2 + n_extra + n_out]
    acc = jnp.dot(a_ref[...], w_ref[...].astype(BF16), preferred_element_type=F32)
    epilogue(acc, extra, outs)


def _proj(a, w, col0, ncols, out_dtypes, epilogue, extras=(), bm=None, bn=None, vmem_mb=52):
    m, k = a.shape
    bm = bm or _pick(m, (1056, 1024, 768, 512, 256, 128, 64, 32, 16, 8))
    bn = bn or _pick(math.gcd(ncols, col0) if col0 else ncols, (512, 256, 128))
    assert col0 % bn == 0 and ncols % bn == 0 and m % bm == 0
    cb = col0 // bn
    in_specs = [pl.BlockSpec((bm, k), lambda i, j: (i, 0)),
                pl.BlockSpec((k, bn), lambda i, j: (0, cb + j))]
    for e in extras:
        in_specs.append(pl.BlockSpec((bm, e.shape[1]), lambda i, j: (i, 0)))
    out_specs = [pl.BlockSpec((bm, bn), lambda i, j: (i, j)) for _ in out_dtypes]
    out_shape = [jax.ShapeDtypeStruct((m, ncols), dt) for dt in out_dtypes]
    kern = functools.partial(_proj_kernel, n_extra=len(extras), n_out=len(out_dtypes), epilogue=epilogue)
    return pl.pallas_call(
        kern,
        grid=(m // bm, ncols // bn),
        in_specs=in_specs,
        out_specs=out_specs,
        out_shape=out_shape,
        compiler_params=_cparams(("arbitrary", "arbitrary"), vmem_mb),
    )(a, w, *extras)


def _ep_store(scale=None):
    def ep(acc, extra, outs):
        v = acc if scale is None else acc * scale
        for o in outs:
            o[...] = v.astype(o.dtype)
    return ep


def _ep_rotate(dk, scale):
    half = dk // 2

    def ep(acc, extra, outs):
        cos, sin = extra[0][...], extra[1][...]
        (o,) = outs
        for h in range(acc.shape[1] // dk):
            x1 = acc[:, h * dk:h * dk + half]
            x2 = acc[:, h * dk + half:(h + 1) * dk]
            r1 = x1 * cos - x2 * sin
            r2 = x2 * cos + x1 * sin
            if scale is not None:
                r1, r2 = r1 * scale, r2 * scale
            o[:, h * dk:h * dk + half] = r1.astype(o.dtype)
            o[:, h * dk + half:(h + 1) * dk] = r2.astype(o.dtype)
    return ep


def _ret_kernel(q_ref, k_ref, v_ref, g_ref, s0_ref, intra_ref, qd_ref, kd_ref, cd_ref,
                o_ref, s_out_ref, s_scr):
    c = pl.program_id(2)

    @pl.when(c == 0)
    def _():
        s_scr[...] = s0_ref[0, 0]

    q, k, v = q_ref[...], k_ref[...], v_ref[...]
    s = s_scr[...]
    att = lax.dot_general(q, k, _NT, preferred_element_type=F32) * intra_ref[0]
    o = jnp.dot(att.astype(BF16), v, preferred_element_type=F32)
    o = o + jnp.dot((q.astype(F32) * qd_ref[0]).astype(BF16), s.astype(BF16), preferred_element_type=F32)
    kd = (k.astype(F32) * kd_ref[0]).astype(BF16)
    s_new = s * cd_ref[0] + lax.dot_general(kd, v, _TN, preferred_element_type=F32)
    s_scr[...] = s_new
    o_ref[...] = (_silu(g_ref[...]) * _layer_norm(o)).astype(o_ref.dtype)

    @pl.when(c == pl.num_programs(2) - 1)
    def _():
        s_out_ref[0, 0] = s_new


def _ret_tables(nh, chunk, n_valid, dv):
    lg = jnp.log1p(-jnp.exp2(-5.0 - jnp.arange(nh, dtype=F32)))
    idx = jnp.arange(chunk, dtype=F32)
    diff = idx[:, None] - idx[None, :]
    intra = jnp.where(diff >= 0, jnp.exp(lg[:, None, None] * jnp.maximum(diff, 0.0)), 0.0)
    q_dec = jnp.exp(lg[:, None] * (idx[None, :] + 1.0))[:, :, None]
    k_dec = jnp.where(idx[None, :] < n_valid, jnp.exp(lg[:, None] * jnp.maximum(n_valid - 1.0 - idx[None, :], 0.0)),
                      0.0)[:, :, None]
    c_dec = jnp.broadcast_to(jnp.exp(lg * n_valid)[:, None, None], (nh, 1, dv))
    return intra, q_dec, k_dec, c_dec


def _retention(q, k, v, g, s0, nb, t, chunk, n_valid, nh, dk, dv):
    nc = t // chunk
    intra, q_dec, k_dec, c_dec = _ret_tables(nh, chunk, n_valid, dv)
    tab = lambda shp: pl.BlockSpec((1,) + shp, lambda b, h, c: (h, 0, 0))
    row = lambda w: pl.BlockSpec((chunk, w), lambda b, h, c: (b * nc + c, h))
    st = pl.BlockSpec((1, 1, dk, dv), lambda b, h, c: (b, h, 0, 0))
    return pl.pallas_call(
        _ret_kernel,
        grid=(nb, nh, nc),
        in_specs=[row(dk), row(dk), row(dv), row(dv), st,
                  tab((chunk, chunk)), tab((chunk, 1)), tab((chunk, 1)), tab((1, dv))],
        out_specs=[row(dv), st],
        out_shape=[jax.ShapeDtypeStruct((nb * t, nh * dv), BF16),
                   jax.ShapeDtypeStruct((nb, nh, dk, dv), F32)],
        scratch_shapes=[pltpu.VMEM((dk, dv), F32)],
        compiler_params=_cparams(("arbitrary", "arbitrary", "arbitrary"), 32),
    )(q, k, v, g, s0, intra, q_dec, k_dec, c_dec)


def _to_key(score):
    bits = pltpu.bitcast(score + 0.0, jnp.int32)
    return jnp.where(bits >= 0, bits, bits ^ jnp.int32(0x7FFFFFFF))


def _fold_lanes(x):
    acc = x[:, :LANE]
    for g in range(1, x.shape[1] // LANE):
        acc = acc + x[:, g * LANE:(g + 1) * LANE]
    return acc


def _kth_largest_key(key_scr, n_tiles, rows, n_sel):
    def count_ge(cand):
        def body(kt, cnt):
            return cnt + _fold_lanes((key_scr[kt] >= cand).astype(jnp.int32))
        cnt = lax.fori_loop(0, n_tiles, body, jnp.zeros((rows, LANE), jnp.int32))
        return jnp.sum(cnt, axis=1, keepdims=True)

    def bit_step(b, t):
        cand = t + jnp.left_shift(jnp.int32(1), 31 - b)
        return jnp.where(count_ge(cand) >= n_sel, cand, t)

    return lax.fori_loop(0, 32, bit_step, jnp.full((rows, 1), INT_MIN, jnp.int32))


def _tie_cutoff(key_scr, n_tiles, rows, w, n_sel, thr):
    def counts(kt, c):
        k = key_scr[kt]
        gt = c[0] + _fold_lanes((k > thr).astype(jnp.int32))
        eq = c[1] + _fold_lanes((k == thr).astype(jnp.int32))
        return gt, eq
    z = jnp.zeros((rows, LANE), jnp.int32)
    gt, eq = lax.fori_loop(0, n_tiles, counts, (z, z))
    need = n_sel - jnp.sum(gt, axis=1, keepdims=True)
    n_eq = jnp.sum(eq, axis=1, keepdims=True)

    def search(_):
        def count_eq_le(cut):
            def body(kt, cnt):
                col = kt * w + lax.broadcasted_iota(jnp.int32, (rows, w), 1)
                hit = jnp.where(col <= cut, (key_scr[kt] == thr).astype(jnp.int32), 0)
                return cnt + _fold_lanes(hit)
            cnt = lax.fori_loop(0, n_tiles, body, jnp.zeros((rows, LANE), jnp.int32))
            return jnp.sum(cnt, axis=1, keepdims=True)

        def bit_step(b, cut):
            cand = cut - jnp.left_shift(jnp.int32(1), 30 - b)
            ok = jnp.logical_and(cand >= 0, count_eq_le(cand) >= need)
            return jnp.where(ok, cand, cut)
        return lax.fori_loop(0, 31, bit_step, jnp.full((rows, 1), 2 ** 31 - 1, jnp.int32))

    excess = jnp.max(jnp.where(n_eq > need, 1, 0)) > 0
    return lax.cond(excess, search, lambda _: jnp.full((rows, 1), 2 ** 31 - 1, jnp.int32), 0)


def _select_bias(key, thr, cut, col, valid):
    bias = jnp.where(key > thr, 0.0, jnp.where(key == thr, jnp.where(col <= cut, 0.0, NEG), NEG))
    return bias if valid is None else jnp.where(valid, bias, NEG)


def _sel_prompt_kernel(iq_ref, ik_ref, iw_ref, bias_ref, key_scr, *, tq, tk, ih, idh, n_sel):
    qi = pl.program_id(0)
    nk_total = bias_ref.shape[1]
    n_tiles = ((qi + 1) * tq + tk - 1) // tk
    row = qi * tq + lax.broadcasted_iota(jnp.int32, (tq, tk), 0)
    lane = lax.broadcasted_iota(jnp.int32, (tq, tk), 1)

    def score_tile(kt, _):
        ik = ik_ref[pl.ds(pl.multiple_of(kt * tk, tk), tk), :]
        acc = jnp.zeros((tq, tk), F32)
        for h in range(ih):
            r = lax.dot_general(iq_ref[:, h * idh:(h + 1) * idh], ik, _NT, preferred_element_type=F32)
            acc = acc + jnp.maximum(r, 0.0) * iw_ref[:, h:h + 1]
        key_scr[kt] = jnp.where(kt * tk + lane <= row, _to_key(acc), INT_MIN)
        return 0
    lax.fori_loop(0, n_tiles, score_tile, 0)

    thr = _kth_largest_key(key_scr, n_tiles, tq, n_sel)
    cut = _tie_cutoff(key_scr, n_tiles, tq, tk, n_sel, thr)

    def emit(kt, _):
        col = kt * tk + lane
        bias_ref[0, kt] = _select_bias(key_scr[kt], thr, cut, col, col <= row)
        return 0
    lax.fori_loop(0, n_tiles, emit, 0)

    def fill(kt, _):
        bias_ref[0, kt] = jnp.full((tq, tk), NEG, F32)
        return 0
    lax.fori_loop(n_tiles, nk_total, fill, 0)


def _select_prompt(iq, ik, iw, t, ih, idh, n_sel, tq, tk):
    nq, nk = t // tq, t // tk
    kern = functools.partial(_sel_prompt_kernel, tq=tq, tk=tk, ih=ih, idh=idh, n_sel=n_sel)
    return pl.pallas_call(
        kern,
        grid=(nq,),
        in_specs=[pl.BlockSpec((tq, ih * idh), lambda i: (i, 0)),
                  pl.BlockSpec((t, idh), lambda i: (0, 0)),
                  pl.BlockSpec((tq, iw.shape[1]), lambda i: (i, 0))],
        out_specs=pl.BlockSpec((1, nk, tq, tk), lambda i: (i, 0, 0, 0)),
        out_shape=jax.ShapeDtypeStruct((nq, nk, tq, tk), F32),
        scratch_shapes=[pltpu.VMEM((nk, tq, tk), jnp.int32)],
        compiler_params=_cparams(("arbitrary",), 48),
    )(iq, ik, iw)


def _attn_prompt_kernel(q_ref, k_ref, v_ref, b_ref, o_ref, m_scr, l_scr, acc_scr, *, tq, tk, nh, dh, scale):
    i, j = pl.program_id(0), pl.program_id(1)
    last = ((i + 1) * tq - 1) // tk

    @pl.when(j == 0)
    def _():
        m_scr[...] = jnp.full(m_scr.shape, -jnp.inf, F32)
        l_scr[...] = jnp.zeros_like(l_scr)
        acc_scr[...] = jnp.zeros_like(acc_scr)

    @pl.when(j <= last)
    def _():
        bias = b_ref[0, 0]
        for h in range(nh):
            sl = slice(h * dh, (h + 1) * dh)
            s = lax.dot_general(q_ref[:, sl], k_ref[:, sl], _NT, preferred_element_type=F32) * scale + bias
            m_prev = m_scr[h]
            m_new = jnp.maximum(m_prev, jnp.max(s, axis=1, keepdims=True))
            alpha = jnp.exp(m_prev - m_new)
            p = jnp.exp(s - m_new)
            l_scr[h] = alpha * l_scr[h] + jnp.sum(p, axis=1, keepdims=True)
            acc_scr[:, sl] = alpha * acc_scr[:, sl] + jnp.dot(p.astype(BF16), v_ref[:, sl],
                                                             preferred_element_type=F32)
            m_scr[h] = m_new

    @pl.when(j == last)
    def _():
        for h in range(nh):
            sl = slice(h * dh, (h + 1) * dh)
            o_ref[:, sl] = (acc_scr[:, sl] / l_scr[h]).astype(o_ref.dtype)


def _attention_prompt(q, k, v, bias, t, nh, dh, tq, tk):
    nq, nk = t // tq, t // tk
    last = lambda i: ((i + 1) * tq - 1) // tk
    kern = functools.partial(_attn_prompt_kernel, tq=tq, tk=tk, nh=nh, dh=dh, scale=dh ** -0.5)
    return pl.pallas_call(
        kern,
        grid=(nq, nk),
        in_specs=[pl.BlockSpec((tq, nh * dh), lambda i, j: (i, 0)),
                  pl.BlockSpec((tk, nh * dh), lambda i, j: (jnp.minimum(j, last(i)), 0)),
                  pl.BlockSpec((tk, nh * dh), lambda i, j: (jnp.minimum(j, last(i)), 0)),
                  pl.BlockSpec((1, 1, tq, tk), lambda i, j: (i, jnp.minimum(j, last(i)), 0, 0))],
        out_specs=pl.BlockSpec((tq, nh * dh), lambda i, j: (i, 0)),
        out_shape=jax.ShapeDtypeStruct((t, nh * dh), BF16),
        scratch_shapes=[pltpu.VMEM((nh, tq, 1), F32), pltpu.VMEM((nh, tq, 1), F32),
                        pltpu.VMEM((tq, nh * dh), F32)],
        compiler_params=_cparams(("arbitrary", "arbitrary"), 40),
    )(q, k, v, bias)


def _sel_sample_kernel(pt_ref, *refs, pp, t, ih, n_sel):
    pages = refs[:pp]
    iq_ref, iw_ref, ikn_ref, bp_ref, bn_ref, key_scr = refs[pp:]
    s = pl.program_id(1)
    n_past = bp_ref.shape[1]
    iq = iq_ref[0]
    iw = iw_ref[0]

    def scores(ik):
        r = lax.dot_general(iq, ik, _NT, preferred_element_type=F32)
        r = jnp.maximum(r, 0.0) * iw
        acc = r[0:t]
        for h in range(1, ih):
            acc = acc + r[h * t:(h + 1) * t]
        return acc

    for p in range(pp):
        key_scr[s * pp + p] = _to_key(scores(pages[p][0].astype(BF16)))

    @pl.when(s == pl.num_programs(1) - 1)
    def _():
        trow = lax.broadcasted_iota(jnp.int32, (t, LANE), 0)
        lane = lax.broadcasted_iota(jnp.int32, (t, LANE), 1)
        new_ok = lane <= trow
        key_scr[n_past] = jnp.where(new_ok, _to_key(scores(ikn_ref[0])), INT_MIN)
        thr = _kth_largest_key(key_scr, n_past + 1, t, n_sel)
        cut = _tie_cutoff(key_scr, n_past + 1, t, LANE, n_sel, thr)

        def emit(kt, _):
            bp_ref[0, kt] = _select_bias(key_scr[kt], thr, cut, kt * LANE + lane, None)
            return 0
        lax.fori_loop(0, n_past, emit, 0)
        bn_ref[0] = _select_bias(key_scr[n_past], thr, cut, n_past * LANE + lane, new_ok)


def _select_sample(page_table, cache_idx_k, iq_rows, iw_rows, ik_new, t, ih, n_sel, pp):
    b, n_pages = page_table.shape
    _, page, idh = cache_idx_k.shape
    assert page == LANE and t <= LANE
    r = ih * t
    page_spec = lambda p: pl.BlockSpec((1, page, idh), lambda i, s, pt: (pt[i, s * pp + p], 0, 0))
    per_b = lambda shp: pl.BlockSpec((1,) + shp, lambda i, s, pt: (i,) + (0,) * len(shp))
    kern = functools.partial(_sel_sample_kernel, pp=pp, t=t, ih=ih, n_sel=n_sel)
    return pl.pallas_call(
        kern,
        grid_spec=pltpu.PrefetchScalarGridSpec(
            num_scalar_prefetch=1,
            grid=(b, n_pages // pp),
            in_specs=[page_spec(p) for p in range(pp)] + [per_b((r, idh)), per_b((r, 1)), per_b((LANE, idh))],
            out_specs=[per_b((n_pages, t, LANE)), per_b((t, LANE))],
            scratch_shapes=[pltpu.VMEM((n_pages + 1, t, LANE), jnp.int32)]),
        out_shape=[jax.ShapeDtypeStruct((b, n_pages, t, LANE), F32),
                   jax.ShapeDtypeStruct((b, t, LANE), F32)],
        compiler_params=_cparams(("arbitrary", "arbitrary"), 32),
    )(page_table, *([cache_idx_k] * pp), iq_rows, iw_rows, ik_new)


def _attn_sample_kernel(pt_ref, *refs, pp, t, nh, dh, scale):
    kp, vp = refs[:pp], refs[pp:2 * pp]
    q_ref, bp_ref, bn_ref, kn_ref, vn_ref, o_ref, kbuf, vbuf, m_scr, l_scr, acc_scr = refs[2 * pp:]
    s = pl.program_id(1)

    @pl.when(s == 0)
    def _():
        m_scr[...] = jnp.full(m_scr.shape, -jnp.inf, F32)
        l_scr[...] = jnp.zeros_like(l_scr)
        acc_scr[...] = jnp.zeros_like(acc_scr)

    def update(k, v, bias):
        sc = lax.dot_general(q_ref[0], k, _NT, preferred_element_type=F32) * scale + bias
        m_prev = m_scr[...]
        m_new = jnp.maximum(m_prev, jnp.max(sc, axis=1, keepdims=True))
        alpha = jnp.exp(m_prev - m_new)
        p = jnp.exp(sc - m_new)
        l_scr[...] = alpha * l_scr[...] + jnp.sum(p, axis=1, keepdims=True)
        acc_scr[...] = alpha * acc_scr[...] + jnp.dot(p.astype(BF16), v, preferred_element_type=F32)
        m_scr[...] = m_new

    for p in range(pp):
        kbuf[p * LANE:(p + 1) * LANE, :] = kp[p][0].astype(BF16)
        vbuf[p * LANE:(p + 1) * LANE, :] = vp[p][0].astype(BF16)
    bias = jnp.concatenate([jnp.tile(bp_ref[0, p], (nh, 1)) for p in range(pp)], axis=1)
    update(kbuf[...], vbuf[...], bias)

    @pl.when(s == pl.num_programs(1) - 1)
    def _():
        update(kn_ref[0], vn_ref[0], jnp.tile(bn_ref[0], (nh, 1)))
        for h in range(nh):
            rows = slice(h * t, (h + 1) * t)
            cols = slice(h * dh, (h + 1) * dh)
            o_ref[0, :, cols] = (acc_scr[rows, cols] / l_scr[rows, :]).astype(o_ref.dtype)


def _attention_sample(page_table, cache_k, cache_v, q_bd, bias_past, bias_new, k_new, v_new, t, nh, dh, pp):
    b, n_pages = page_table.shape
    _, page, w = cache_k.shape
    r = nh * t
    page_spec = lambda p: pl.BlockSpec((1, page, w), lambda i, s, pt: (pt[i, s * pp + p], 0, 0))
    per_b = lambda shp: pl.BlockSpec((1,) + shp, lambda i, s, pt: (i,) + (0,) * len(shp))
    kern = functools.partial(_attn_sample_kernel, pp=pp, t=t, nh=nh, dh=dh, scale=dh ** -0.5)
    return pl.pallas_call(
        kern,
        grid_spec=pltpu.PrefetchScalarGridSpec(
            num_scalar_prefetch=1,
            grid=(b, n_pages // pp),
            in_specs=[page_spec(p) for p in range(pp)] * 2 + [
                per_b((r, w)),
                pl.BlockSpec((1, pp, t, LANE), lambda i, s, pt: (i, s, 0, 0)),
                per_b((t, LANE)), per_b((LANE, w)), per_b((LANE, w))],
            out_specs=per_b((t, w)),
            scratch_shapes=[pltpu.VMEM((pp * page, w), BF16), pltpu.VMEM((pp * page, w), BF16),
                            pltpu.VMEM((r, 1), F32), pltpu.VMEM((r, 1), F32), pltpu.VMEM((r, w), F32)]),
        out_shape=jax.ShapeDtypeStruct((b, t, w), BF16),
        compiler_params=_cparams(("arbitrary", "arbitrary"), 48),
    )(page_table, *([cache_k] * pp), *([cache_v] * pp), q_bd, bias_past, bias_new, k_new, v_new)


def _merge_kernel(ra_ref, wr_ref, at_ref, wa_ref, ga_ref, gb_ref, o_ref):
    a = jnp.dot(ra_ref[...], wr_ref[...].astype(BF16), preferred_element_type=F32)
    b = jnp.dot(at_ref[...], wa_ref[...].astype(BF16), preferred_element_type=F32)
    o_ref[...] = (jax.nn.sigmoid(ga_ref[...]) * a + jax.nn.sigmoid(gb_ref[...]) * b).astype(o_ref.dtype)


def _merge(ra, w_ret_o, att, w_att_o, ga, gb):
    m, kr = ra.shape
    ka = att.shape[1]
    d = w_ret_o.shape[1]
    bm = _pick(m, (528, 512, 256, 128, 64, 32, 16, 8))
    bn = _pick(d, (512, 256, 128))
    row = lambda w: pl.BlockSpec((bm, w), lambda i, j: (i, 0))
    col = lambda kk: pl.BlockSpec((kk, bn), lambda i, j: (0, j))
    tile = pl.BlockSpec((bm, bn), lambda i, j: (i, j))
    return pl.pallas_call(
        _merge_kernel,
        grid=(m // bm, d // bn),
        in_specs=[row(kr), col(kr), row(ka), col(ka), tile, tile],
        out_specs=tile,
        out_shape=jax.ShapeDtypeStruct((m, d), BF16),
        compiler_params=_cparams(("arbitrary", "arbitrary"), 52),
    )(ra, w_ret_o, att, w_att_o, ga, gb)


def _router_kernel(h_ref, w_ref, b_ref, ti_ref, tw_ref, *, n_exp):
    s = jax.nn.sigmoid(jnp.dot(h_ref[...].astype(BF16), w_ref[...].astype(BF16), preferred_element_type=F32))
    sb = s + b_ref[...]
    tm = s.shape[0]
    lane = lax.broadcasted_iota(jnp.int32, (tm, n_exp), 1)
    per = n_exp // N_GROUPS
    grp = lane // per
    ninf = -jnp.inf

    def first_argmax(x):
        mx = jnp.max(x, axis=1, keepdims=True)
        idx = jnp.min(jnp.where(x == mx, lane, n_exp), axis=1, keepdims=True)
        return mx, idx

    gscore = []
    for g in range(N_GROUPS):
        xg = jnp.where(grp == g, sb, ninf)
        m1, i1 = first_argmax(xg)
        m2 = jnp.max(jnp.where(lane == i1, ninf, xg), axis=1, keepdims=True)
        gscore.append(m1 + m2)
    keep = jnp.zeros((tm, n_exp), jnp.int32)
    for g in range(N_GROUPS):
        rank = jnp.zeros((tm, 1), jnp.int32)
        for o in range(N_GROUPS):
            if o == g:
                continue
            ahead = (gscore[o] > gscore[g]) if o > g else (gscore[o] >= gscore[g])
            rank = rank + ahead.astype(jnp.int32)
        keep = jnp.where(grp == g, (rank < TOPK_GROUPS).astype(jnp.int32), keep)
    x = jnp.where(keep > 0, sb, ninf)
    olane = lax.broadcasted_iota(jnp.int32, (tm, LANE), 1)
    ti = jnp.zeros((tm, LANE), jnp.int32)
    tw = jnp.zeros((tm, LANE), F32)
    tot = jnp.zeros((tm, 1), F32)
    for j in range(MOE_TOPK):
        _, idx = first_argmax(x)
        sj = jnp.sum(jnp.where(lane == idx, s, 0.0), axis=1, keepdims=True)
        x = jnp.where(lane == idx, ninf, x)
        ti = jnp.where(olane == j, idx, ti)
        tw = jnp.where(olane == j, sj, tw)
        tot = tot + sj
    ti_ref[...] = ti
    tw_ref[...] = tw / tot * ROUTED_SCALE


def _route(h, w_router, b_router):
    m, d = h.shape
    n_exp = w_router.shape[1]
    tm = _pick(m, (256, 128, 64, 32, 16, 8))
    ti, tw = pl.pallas_call(
        functools.partial(_router_kernel, n_exp=n_exp),
        grid=(m // tm,),
        in_specs=[pl.BlockSpec((tm, d), lambda i: (i, 0)),
                  pl.BlockSpec((d, n_exp), lambda i: (0, 0)),
                  pl.BlockSpec((1, n_exp), lambda i: (0, 0))],
        out_specs=[pl.BlockSpec((tm, LANE), lambda i: (i, 0))] * 2,
        out_shape=[jax.ShapeDtypeStruct((m, LANE), jnp.int32), jax.ShapeDtypeStruct((m, LANE), F32)],
        compiler_params=_cparams(("arbitrary",), 32),
    )(h, w_router, b_router.reshape(1, n_exp))
    return ti[:, :MOE_TOPK], tw[:, :MOE_TOPK]


def _expert_kernel(be_ref, nu_ref, tok_hbm, h_hbm, wg_ref, wu_ref, wd_ref, y_ref,
                   x_buf, idx_smem, row_sem, idx_sem, *, bm):
    i, hf = pl.program_id(0), pl.program_id(1)
    n_used = nu_ref[0]

    def row_copy(blk_slot, r, tok):
        return pltpu.make_async_copy(h_hbm.at[pl.ds(tok, 1)], x_buf.at[blk_slot, pl.ds(r, 1)],
                                     row_sem.at[blk_slot])

    def start_gather(blk):
        slot = blk % 2
        cp = pltpu.make_async_copy(tok_hbm.at[blk], idx_smem, idx_sem)
        cp.start()
        cp.wait()

        def body(r, _):
            row_copy(slot, r, idx_smem[r]).start()
            return 0
        lax.fori_loop(0, bm, body, 0)

    def wait_gather(blk):
        slot = blk % 2

        def body(r, _):
            row_copy(slot, r, 0).wait()
            return 0
        lax.fori_loop(0, bm, body, 0)

    @pl.when(jnp.logical_and(i == 0, hf == 0))
    def _():
        start_gather(0)

    @pl.when(jnp.logical_and(i < n_used, hf == 0))
    def _():
        wait_gather(i)

    @pl.when(jnp.logical_and(i + 1 < n_used, hf == 1))
    def _():
        start_gather(i + 1)

    @pl.when(i < n_used)
    def _():
        x = x_buf[i % 2].astype(BF16)
        g = jnp.dot(x, wg_ref[0].astype(BF16), preferred_element_type=F32)
        u = jnp.dot(x, wu_ref[0].astype(BF16), preferred_element_type=F32)
        a = (_silu(g) * u).astype(BF16)
        y = jnp.dot(a, wd_ref[0].astype(BF16), preferred_element_type=F32)

        @pl.when(hf == 0)
        def _():
            y_ref[...] = y

        @pl.when(hf == 1)
        def _():
            y_ref[...] = y_ref[...] + y


def _experts(h, block_e, n_used, row_tok, w_gate, w_up, w_down, bm):
    m, d = h.shape
    n_exp, _, de = w_gate.shape
    nb = block_e.shape[0]
    deh = de // 2
    last_used = lambda i, nu: jnp.minimum(i, nu[0] - 1)
    return pl.pallas_call(
        functools.partial(_expert_kernel, bm=bm),
        grid_spec=pltpu.PrefetchScalarGridSpec(
            num_scalar_prefetch=2,
            grid=(nb, 2),
            in_specs=[pl.BlockSpec(memory_space=pl.ANY),
                      pl.BlockSpec(memory_space=pl.ANY),
                      pl.BlockSpec((1, d, deh), lambda i, hf, be, nu: (be[i], 0, hf)),
                      pl.BlockSpec((1, d, deh), lambda i, hf, be, nu: (be[i], 0, hf)),
                      pl.BlockSpec((1, deh, d), lambda i, hf, be, nu: (be[i], hf, 0))],
            out_specs=pl.BlockSpec((bm, d), lambda i, hf, be, nu: (last_used(i, nu), 0)),
            scratch_shapes=[pltpu.VMEM((2, bm, d), F32), pltpu.SMEM((bm,), jnp.int32),
                            pltpu.SemaphoreType.DMA((2,)), pltpu.SemaphoreType.DMA(())]),
        out_shape=jax.ShapeDtypeStruct((nb * bm, d), F32),
        compiler_params=_cparams(("arbitrary", "arbitrary"), 56),
    )(block_e, n_used, row_tok, h, w_gate, w_up, w_down)


def _dispatch(top_i, n_exp, bm):
    m = top_i.shape[0]
    n_assign = m * MOE_TOPK
    nb = -(-n_assign // bm) + n_exp
    flat_e = top_i.reshape(-1)
    order = jnp.argsort(flat_e, stable=True).astype(jnp.int32)
    se = flat_e[order]
    counts = jnp.bincount(flat_e, length=n_exp).astype(jnp.int32)
    nblk = (counts + bm - 1) // bm
    blk_end = jnp.cumsum(nblk)
    blk_start = blk_end - nblk
    n_used = blk_end[-1]
    start = jnp.cumsum(counts) - counts
    dest = blk_start[se] * bm + jnp.arange(n_assign, dtype=jnp.int32) - start[se]
    row_tok = jnp.zeros((nb * bm,), jnp.int32).at[dest].set(order // MOE_TOPK)
    pos = jnp.zeros((n_assign,), jnp.int32).at[order].set(dest)
    blk = jnp.arange(nb, dtype=jnp.int32)
    be = jnp.minimum(jnp.searchsorted(blk_end, blk, side='right'), n_exp - 1).astype(jnp.int32)
    be = jnp.where(blk < n_used, be, be[jnp.maximum(n_used - 1, 0)])
    return be, n_used.reshape(1).astype(jnp.int32), row_tok.reshape(nb, bm), pos.reshape(m, MOE_TOPK)


def _glu_kernel(h_ref, wg_ref, wu_ref, o_ref):
    x = h_ref[...].astype(BF16)
    g = jnp.dot(x, wg_ref[...].astype(BF16), preferred_element_type=F32)
    u = jnp.dot(x, wu_ref[...].astype(BF16), preferred_element_type=F32)
    o_ref[...] = (_silu(g) * u).astype(o_ref.dtype)


def _glu(h, wg, wu):
    m, d = h.shape
    de = wg.shape[1]
    bm = _pick(m, (528, 512, 256, 128, 64, 32, 16, 8))
    bn = _pick(de, (256, 128))
    return pl.pallas_call(
        _glu_kernel,
        grid=(m // bm, de // bn),
        in_specs=[pl.BlockSpec((bm, d), lambda i, j: (i, 0)),
                  pl.BlockSpec((d, bn), lambda i, j: (0, j)),
                  pl.BlockSpec((d, bn), lambda i, j: (0, j))],
        out_specs=pl.BlockSpec((bm, bn), lambda i, j: (i, j)),
        out_shape=jax.ShapeDtypeStruct((m, de), BF16),
        compiler_params=_cparams(("arbitrary", "arbitrary"), 48),
    )(h, wg, wu)


def _combine_kernel(pos_hbm, y_hbm, tw_ref, sh_ref, x_ref, gate_ref, gam_ref, bet_ref, o_ref,
                    buf, idx_smem, row_sem, idx_sem, *, tt):
    b, j = pl.program_id(0), pl.program_id(1)
    step = b * pl.num_programs(1) + j
    n = tt * MOE_TOPK
    cp = pltpu.make_async_copy(pos_hbm.at[step], idx_smem, idx_sem)
    cp.start()
    cp.wait()

    def row_copy(a, src):
        return pltpu.make_async_copy(y_hbm.at[pl.ds(src, 1)], buf.at[a % MOE_TOPK, pl.ds(a // MOE_TOPK, 1)],
                                     row_sem)

    def start(a, _):
        row_copy(a, idx_smem[a]).start()
        return 0
    lax.fori_loop(0, n, start, 0)

    def wait(a, _):
        row_copy(a, 0).wait()
        return 0
    lax.fori_loop(0, n, wait, 0)

    tw = tw_ref[0]
    routed = buf[0] * tw[:, 0:1]
    for e in range(1, MOE_TOPK):
        routed = routed + buf[e] * tw[:, e:e + 1]
    z = DN_ALPHA * x_ref[0] + gate_ref[0] * (routed + sh_ref[0])
    o_ref[0] = _layer_norm(z) * gam_ref[...] + bet_ref[...]


def _combine(pos, y_sorted, top_w, shared, x1, gate, gamma, beta):
    b, t, d = x1.shape
    tt = _pick(t, (64, 32, 16, 8))
    n = tt * MOE_TOPK
    pos_steps = pos.reshape(-1, n)
    row = lambda w: pl.BlockSpec((1, tt, w), lambda i, j: (i, j, 0))
    mod = pl.BlockSpec((1, 1, d), lambda i, j: (i, 0, 0))
    vec = pl.BlockSpec((1, d), lambda i, j: (0, 0))
    return pl.pallas_call(
        functools.partial(_combine_kernel, tt=tt),
        grid=(b, t // tt),
        in_specs=[pl.BlockSpec(memory_space=pl.ANY), pl.BlockSpec(memory_space=pl.ANY),
                  row(MOE_TOPK), row(d), row(d), mod, vec, vec],
        out_specs=row(d),
        out_shape=jax.ShapeDtypeStruct((b, t, d), F32),
        scratch_shapes=[pltpu.VMEM((MOE_TOPK, tt, d), F32), pltpu.SMEM((n,), jnp.int32),
                        pltpu.SemaphoreType.DMA(()), pltpu.SemaphoreType.DMA(())],
        compiler_params=_cparams(("arbitrary", "arbitrary"), 40),
    )(pos_steps, y_sorted, top_w, shared, x1, gate, gamma.reshape(1, d), beta.reshape(1, d))


def kernel(x_prompt, x_sample, c_prompt, c_sample, cache_k, cache_v, cache_idx_k, state_ret, page_table, w_cond, b_cond, w_in, w_ret_o, w_att_o, w_out, ln_mix_g, ln_mix_b, w_router, b_router, w_exp_gate, w_exp_up, w_exp_down, w_sh_gate, w_sh_up, w_sh_down, ln_ffn_g, ln_ffn_b):
    bp, tp, d = x_prompt.shape
    bs, ts, _ = x_sample.shape
    assert bp == 1, "prompt group kernels are written for a single prompt sequence"
    n_pool, page, ah, dh = cache_k.shape
    idh = cache_idx_k.shape[2]
    rh, dk, dv = state_ret.shape[1:]
    n_pages = page_table.shape[1]
    past = n_pages * page
    n_exp = w_router.shape[1]
    aw, rqk, rv_w = ah * dh, rh * dk, rh * dv
    d_in = w_in.shape[1]
    ih = d_in - (2 * rqk + 2 * rv_w + 3 * aw + idh + 2 * d)
    ih = ih // (idh + 1)
    iq_w = ih * idh
    widths = (rqk, rqk, rv_w, rv_w, aw, aw, aw, iq_w, idh, ih, d, d)
    assert sum(widths) == d_in
    offs = np.concatenate([[0], np.cumsum(widths)]).tolist()
    o_rq, o_rk, o_rv, o_rg, o_aq, o_ak, o_av, o_iq, o_ik, o_iw, o_ga, o_gb = offs[:12]
    mp, ms = bp * tp, bs * ts
    m = mp + ms

    mod = _modulation(jnp.concatenate([c_prompt, c_sample], axis=0), w_cond, b_cond)
    mods_p = [v[:, None, :] for v in jnp.split(mod[:bp], 6, axis=-1)]
    mods_s = [v[:, None, :] for v in jnp.split(mod[bp:], 6, axis=-1)]

    h = jnp.concatenate([_ln_mod(x_prompt, mods_p[1], mods_p[0], BF16).reshape(mp, d),
                         _ln_mod(x_sample, mods_s[1], mods_s[0], BF16).reshape(ms, d)], axis=0)

    pos = jnp.concatenate([jnp.tile(jnp.arange(tp, dtype=jnp.int32), bp),
                           jnp.tile(past + jnp.arange(ts, dtype=jnp.int32), bs)]).astype(F32)
    half = dk // 2
    inv = ROT_BASE ** (-jnp.arange(half, dtype=F32) / half)
    ang = pos[:, None] * inv[None, :]
    cos, sin = jnp.cos(ang), jnp.sin(ang)

    rq, = _proj(h, w_in, o_rq, rqk, [BF16], _ep_rotate(dk, None), extras=(cos, sin))
    rk, = _proj(h, w_in, o_rk, rqk, [BF16], _ep_rotate(dk, dk ** -0.5), extras=(cos, sin))
    rv, = _proj(h, w_in, o_rv, rv_w, [BF16], _ep_store())
    rg, = _proj(h, w_in, o_rg, rv_w, [F32], _ep_store())
    aq, = _proj(h, w_in, o_aq, aw, [BF16], _ep_store())
    ak, ak_b = _proj(h, w_in, o_ak, aw, [F32, BF16], _ep_store())
    av, av_b = _proj(h, w_in, o_av, aw, [F32, BF16], _ep_store())
    iq, = _proj(h, w_in, o_iq, iq_w, [BF16], _ep_store(idh ** -0.5))
    ik, ik_b = _proj(h, w_in, o_ik, idh, [F32, BF16], _ep_store(), bn=LANE)
    iw, = _proj(h, w_in, o_iw, LANE, [F32], _ep_store(ih ** -0.5), bn=LANE)
    w_gates = w_in[:, o_ga:]
    ga, = _proj(h, w_gates, 0, d, [F32], _ep_store())
    gb, = _proj(h, w_gates, d, d, [F32], _ep_store())

    chunk = _pick(tp, (256, 128))
    ra_p, s_p = _retention(rq, rk, rv, rg, jnp.zeros((bp, rh, dk, dv), F32), bp, tp, chunk, chunk, rh, dk, dv)
    pad_s = lambda a: jnp.pad(a[mp:].reshape(bs, ts, -1), ((0, 0), (0, LANE - ts), (0, 0))).reshape(bs * LANE, -1)
    ra_s, s_s = _retention(pad_s(rq), pad_s(rk), pad_s(rv), pad_s(rg), state_ret.astype(F32),
                           bs, LANE, LANE, ts, rh, dk, dv)
    ra_s = ra_s.reshape(bs, LANE, rv_w)[:, :ts]

    tq = _pick(tp, (256, 128))
    tk = _pick(tp, (512, 256, 128))
    bias_p = _select_prompt(iq, ik_b[:mp], iw, tp, ih, idh, min(TOPK_MAX, tp // 4), tq, tk)
    att_p = _attention_prompt(aq, ak_b, av_b, bias_p, tp, ah, dh, tq, tk)

    pp = _pick(n_pages, (4, 2, 1))
    hm = lambda a, nh: a.reshape(bs, ts, nh, -1).transpose(0, 2, 1, 3).reshape(bs, nh * ts, -1)
    iq_rows = hm(iq[mp:], ih)
    iw_rows = iw[mp:, :ih].reshape(bs, ts, ih).transpose(0, 2, 1).reshape(bs, ih * ts, 1)
    pad_new = lambda a: jnp.pad(a[mp:].reshape(bs, ts, -1), ((0, 0), (0, LANE - ts), (0, 0)))
    bias_past, bias_new = _select_sample(page_table, cache_idx_k, iq_rows, iw_rows, pad_new(ik_b),
                                         ts, ih, min(TOPK_MAX, (past + ts) // 4), pp)
    q_heads = aq[mp:].reshape(bs, ts, ah, dh).transpose(0, 2, 1, 3)
    q_bd = (q_heads[:, :, :, None, :] * jnp.eye(ah, dtype=BF16)[None, :, None, :, None]).reshape(bs, ah * ts, aw)
    att_s = _attention_sample(page_table, cache_k.reshape(n_pool, page, aw), cache_v.reshape(n_pool, page, aw),
                              q_bd, bias_past, bias_new, pad_new(ak_b), pad_new(av_b), ts, ah, dh, pp)

    ra = jnp.concatenate([ra_p, ra_s.reshape(ms, rv_w)], axis=0)
    att = jnp.concatenate([att_p, att_s.reshape(ms, aw)], axis=0)
    u = _merge(ra, w_ret_o, att, w_att_o, ga, gb)
    mix, = _proj(u, w_out, 0, d, [F32], _ep_store())
    x1_p, h2_p = _post_ln(x_prompt, mix.reshape(1, m, d), mods_p[2], ln_mix_g, ln_mix_b, mods_p[4], mods_p[3])
    x1_s, h2_s = _post_ln(x_sample, mix[mp:].reshape(bs, ts, d), mods_s[2], ln_mix_g, ln_mix_b, mods_s[4], mods_s[3])
    h2 = jnp.concatenate([h2_p.reshape(mp, d), h2_s.reshape(ms, d)], axis=0)

    top_i, top_w = _route(h2, w_router, b_router)
    n_assign = m * MOE_TOPK
    bm = 320 if n_assign // n_exp >= 256 else _pick(max(n_assign // n_exp, 8), (128, 64, 32, 16, 8))
    block_e, n_used, row_tok, dest = _dispatch(top_i, n_exp, bm)
    y_sorted = _experts(h2, block_e, n_used, row_tok, w_exp_gate, w_exp_up, w_exp_down, bm)
    shared, = _proj(_glu(h2, w_sh_gate, w_sh_up), w_sh_down, 0, d, [F32], _ep_store())

    y_p = _combine(dest[:mp], y_sorted, top_w[:mp].reshape(bp, tp, MOE_TOPK), shared.reshape(1, m, d),
                   x1_p, mods_p[5], ln_ffn_g, ln_ffn_b)
    y_s = _combine(dest[mp:], y_sorted, top_w[mp:].reshape(bs, ts, MOE_TOPK), shared[mp:].reshape(bs, ts, d),
                   x1_s, mods_s[5], ln_ffn_g, ln_ffn_b)

    return (y_p, y_s,
            ak[:mp].reshape(bp, tp, ah, dh), av[:mp].reshape(bp, tp, ah, dh), ik[:mp].reshape(bp, tp, idh),
            s_p.reshape(bp, rh, dk, dv),
            ak[mp:].reshape(bs, ts, ah, dh), av[mp:].reshape(bs, ts, ah, dh), ik[mp:].reshape(bs, ts, idh),
            s_s)
```

```python
import functools
import math

import jax
import jax.numpy as jnp
import numpy as np
from jax import lax
from jax.experimental import pallas as pl
from jax.experimental.pallas import tpu as pltpu

ROT_BASE = 10000.0
TOPK_MAX = 256
Q_BLOCK = 128
MOE_TOPK = 8
N_GROUPS = 8
TOPK_GROUPS = 4
ROUTED_SCALE = 2.5
LN_EPS = 1e-5
DEPTH = 1
DN_ALPHA = (2.0 * DEPTH) ** 0.25

LANE = 128
SUBLANE = 8
VMEM_BYTES_V7X = 64 * 2 ** 20

DMA_UNROLL = 8

NEG = -1e30
INT_MIN = -2 ** 31
BF16 = jnp.bfloat16
F32 = jnp.float32

_NT = (((1,), (1,)), ((), ()))
_TN = (((0,), (0,)), ((), ()))


def _cparams(sem, vmem_mb):
    assert vmem_mb * 2 ** 20 < VMEM_BYTES_V7X
    return pltpu.CompilerParams(dimension_semantics=sem, vmem_limit_bytes=vmem_mb * 2 ** 20)


def _pick(n, cands):
    for c in cands:
        if c <= n and n % c == 0:
            return c
    return n


def _silu(x):
    return x * jax.nn.sigmoid(x)


def _layer_norm(x):
    mu = jnp.mean(x, axis=-1, keepdims=True)
    xc = x - mu
    var = jnp.mean(xc * xc, axis=-1, keepdims=True)
    return xc * lax.rsqrt(var + LN_EPS)


def _cond_kernel(c_ref, w_ref, b_ref, o_ref):
    a = _silu(c_ref[...]).astype(BF16)
    o_ref[...] = jnp.dot(a, w_ref[...].astype(BF16), preferred_element_type=F32) + b_ref[...]


def _modulation(c, w_cond, b_cond):
    nb, d = c.shape
    n = w_cond.shape[1]
    mp = -(-nb // SUBLANE) * SUBLANE
    cp = jnp.pad(c, ((0, mp - nb), (0, 0)))
    tn = _pick(n, (512, 256, 128))
    out = pl.pallas_call(
        _cond_kernel,
        grid=(n // tn,),
        in_specs=[pl.BlockSpec((mp, d), lambda j: (0, 0)),
                  pl.BlockSpec((d, tn), lambda j: (0, j)),
                  pl.BlockSpec((1, tn), lambda j: (0, j))],
        out_specs=pl.BlockSpec((mp, tn), lambda j: (0, j)),
        out_shape=jax.ShapeDtypeStruct((mp, n), F32),
        compiler_params=_cparams(("arbitrary",), 40),
    )(cp, w_cond, b_cond.reshape(1, n))
    return out[:nb]


def _ln_mod_kernel(x_ref, sc_ref, sh_ref, o_ref):
    y = _layer_norm(x_ref[0])
    o_ref[0] = (y * (1.0 + sc_ref[0]) + sh_ref[0]).astype(o_ref.dtype)


def _ln_mod(x, scale, shift, out_dtype):
    b, t, d = x.shape
    tt = _pick(t, (256, 128, 64, 32, 16, 8))
    mod = pl.BlockSpec((1, 1, d), lambda i, j: (i, 0, 0))
    row = pl.BlockSpec((1, tt, d), lambda i, j: (i, j, 0))
    return pl.pallas_call(
        _ln_mod_kernel,
        grid=(b, t // tt),
        in_specs=[row, mod, mod],
        out_specs=row,
        out_shape=jax.ShapeDtypeStruct((b, t, d), out_dtype),
        compiler_params=_cparams(("arbitrary", "arbitrary"), 40),
    )(x, scale, shift)


def _post_ln_kernel(x_ref, y_ref, gate_ref, gam_ref, bet_ref, sc_ref, sh_ref, x1_ref, h_ref):
    z = DN_ALPHA * x_ref[0] + gate_ref[0] * y_ref[0]
    x1 = _layer_norm(z) * gam_ref[...] + bet_ref[...]
    x1_ref[0] = x1
    h_ref[0] = (_layer_norm(x1) * (1.0 + sc_ref[0]) + sh_ref[0]).astype(h_ref.dtype)


def _post_ln(x, y, gate, gamma, beta, scale, shift):
    b, t, d = x.shape
    tt = _pick(t, (256, 128, 64, 32, 16, 8))
    mod = pl.BlockSpec((1, 1, d), lambda i, j: (i, 0, 0))
    row = pl.BlockSpec((1, tt, d), lambda i, j: (i, j, 0))
    vec = pl.BlockSpec((1, d), lambda i, j: (0, 0))
    return pl.pallas_call(
        _post_ln_kernel,
        grid=(b, t // tt),
        in_specs=[row, row, mod, vec, vec, mod, mod],
        out_specs=[row, row],
        out_shape=[jax.ShapeDtypeStruct((b, t, d), F32), jax.ShapeDtypeStruct((b, t, d), F32)],
        compiler_params=_cparams(("arbitrary", "arbitrary"), 48),
    )(x, y, gate, gamma.reshape(1, d), beta.reshape(1, d), scale, shift)


def _proj_kernel(*refs, n_extra, n_out, epilogue):
    a_ref, w_ref = refs[0], refs[1]
    extra = refs[2:2 + n_extra]
    outs = refs[2 + n_extra:2 + n_extra + n_out]
    acc = jnp.dot(a_ref[...], w_ref[...].astype(BF16), preferred_element_type=F32)
    epilogue(acc, extra, outs)


def _proj(a, w, col0, ncols, out_dtypes, epilogue, extras=(), bm=None, bn=None, vmem_mb=52):
    m, k = a.shape
    bm = bm or _pick(m, (1056, 1024, 768, 512, 256, 128, 64, 32, 16, 8))
    bn = bn or _pick(math.gcd(ncols, col0) if col0 else ncols, (512, 256, 128))
    assert col0 % bn == 0 and ncols % bn == 0 and m % bm == 0
    cb = col0 // bn
    in_specs = [pl.BlockSpec((bm, k), lambda i, j: (i, 0)),
                pl.BlockSpec((k, bn), lambda i, j: (0, cb + j))]
    for e in extras:
        in_specs.append(pl.BlockSpec((bm, e.shape[1]), lambda i, j: (i, 0)))
    out_specs = [pl.BlockSpec((bm, bn), lambda i, j: (i, j)) for _ in out_dtypes]
    out_shape = [jax.ShapeDtypeStruct((m, ncols), dt) for dt in out_dtypes]
    kern = functools.partial(_proj_kernel, n_extra=len(extras), n_out=len(out_dtypes), epilogue=epilogue)
    return pl.pallas_call(
        kern,
        grid=(m // bm, ncols // bn),
        in_specs=in_specs,
        out_specs=out_specs,
        out_shape=out_shape,
        compiler_params=_cparams(("arbitrary", "arbitrary"), vmem_mb),
    )(a, w, *extras)


def _ep_store(scale=None):
    def ep(acc, extra, outs):
        v = acc if scale is None else acc * scale
        for o in outs:
            o[...] = v.astype(o.dtype)
    return ep


def _ep_rotate(dk, scale):
    half = dk // 2

    def ep(acc, extra, outs):
        cos, sin = extra[0][...], extra[1][...]
        (o,) = outs
        for h in range(acc.shape[1] // dk):
            x1 = acc[:, h * dk:h * dk + half]
            x2 = acc[:, h * dk + half:(h + 1) * dk]
            r1 = x1 * cos - x2 * sin
            r2 = x2 * cos + x1 * sin
            if scale is not None:
                r1, r2 = r1 * scale, r2 * scale
            o[:, h * dk:h * dk + half] = r1.astype(o.dtype)
            o[:, h * dk + half:(h + 1) * dk] = r2.astype(o.dtype)
    return ep


def _ret_kernel(q_ref, k_ref, v_ref, g_ref, s0_ref, intra_ref, qd_ref, kd_ref, cd_ref,
                o_ref, s_out_ref, s_scr):
    c = pl.program_id(2)

    @pl.when(c == 0)
    def _():
        s_scr[...] = s0_ref[0, 0]

    q, k, v = q_ref[...], k_ref[...], v_ref[...]
    s = s_scr[...]
    att = lax.dot_general(q, k, _NT, preferred_element_type=F32) * intra_ref[0]
    o = jnp.dot(att.astype(BF16), v, preferred_element_type=F32)
    o = o + jnp.dot((q.astype(F32) * qd_ref[0]).astype(BF16), s.astype(BF16), preferred_element_type=F32)
    kd = (k.astype(F32) * kd_ref[0]).astype(BF16)
    s_new = s * cd_ref[0] + lax.dot_general(kd, v, _TN, preferred_element_type=F32)
    s_scr[...] = s_new
    o_ref[...] = (_silu(g_ref[...]) * _layer_norm(o)).astype(o_ref.dtype)

    @pl.when(c == pl.num_programs(2) - 1)
    def _():
        s_out_ref[0, 0] = s_new


def _ret_tables(nh, chunk, n_valid, dv):
    lg = jnp.log1p(-jnp.exp2(-5.0 - jnp.arange(nh, dtype=F32)))
    idx = jnp.arange(chunk, dtype=F32)
    diff = idx[:, None] - idx[None, :]
    intra = jnp.where(diff >= 0, jnp.exp(lg[:, None, None] * jnp.maximum(diff, 0.0)), 0.0)
    q_dec = jnp.exp(lg[:, None] * (idx[None, :] + 1.0))[:, :, None]
    k_dec = jnp.where(idx[None, :] < n_valid, jnp.exp(lg[:, None] * jnp.maximum(n_valid - 1.0 - idx[None, :], 0.0)),
                      0.0)[:, :, None]
    c_dec = jnp.broadcast_to(jnp.exp(lg * n_valid)[:, None, None], (nh, 1, dv))
    return intra, q_dec, k_dec, c_dec


def _retention(q, k, v, g, s0, nb, t, chunk, n_valid, nh, dk, dv):
    nc = t // chunk
    intra, q_dec, k_dec, c_dec = _ret_tables(nh, chunk, n_valid, dv)
    tab = lambda shp: pl.BlockSpec((1,) + shp, lambda b, h, c: (h, 0, 0))
    row = lambda w: pl.BlockSpec((chunk, w), lambda b, h, c: (b * nc + c, h))
    st = pl.BlockSpec((1, 1, dk, dv), lambda b, h, c: (b, h, 0, 0))
    return pl.pallas_call(
        _ret_kernel,
        grid=(nb, nh, nc),
        in_specs=[row(dk), row(dk), row(dv), row(dv), st,
                  tab((chunk, chunk)), tab((chunk, 1)), tab((chunk, 1)), tab((1, dv))],
        out_specs=[row(dv), st],
        out_shape=[jax.ShapeDtypeStruct((nb * t, nh * dv), BF16),
                   jax.ShapeDtypeStruct((nb, nh, dk, dv), F32)],
        scratch_shapes=[pltpu.VMEM((dk, dv), F32)],
        compiler_params=_cparams(("arbitrary", "arbitrary", "arbitrary"), 32),
    )(q, k, v, g, s0, intra, q_dec, k_dec, c_dec)


def _to_key(score):
    bits = pltpu.bitcast(score + 0.0, jnp.int32)
    return jnp.where(bits >= 0, bits, bits ^ jnp.int32(0x7FFFFFFF))


def _fold_lanes(x):
    acc = x[:, :LANE]
    for g in range(1, x.shape[1] // LANE):
        acc = acc + x[:, g * LANE:(g + 1) * LANE]
    return acc


def _kth_largest_key(key_scr, n_tiles, rows, n_sel):
    def count_ge(cand):
        def body(kt, cnt):
            return cnt + _fold_lanes((key_scr[kt] >= cand).astype(jnp.int32))
        cnt = lax.fori_loop(0, n_tiles, body, jnp.zeros((rows, LANE), jnp.int32))
        return jnp.sum(cnt, axis=1, keepdims=True)

    def bit_step(b, t):
        cand = t + jnp.left_shift(jnp.int32(1), 31 - b)
        return jnp.where(count_ge(cand) >= n_sel, cand, t)

    return lax.fori_loop(0, 32, bit_step, jnp.full((rows, 1), INT_MIN, jnp.int32))


def _tie_cutoff(key_scr, n_tiles, rows, w, n_sel, thr):
    def counts(kt, c):
        k = key_scr[kt]
        gt = c[0] + _fold_lanes((k > thr).astype(jnp.int32))
        eq = c[1] + _fold_lanes((k == thr).astype(jnp.int32))
        return gt, eq
    z = jnp.zeros((rows, LANE), jnp.int32)
    gt, eq = lax.fori_loop(0, n_tiles, counts, (z, z))
    need = n_sel - jnp.sum(gt, axis=1, keepdims=True)
    n_eq = jnp.sum(eq, axis=1, keepdims=True)

    def search(_):
        def count_eq_le(cut):
            def body(kt, cnt):
                col = kt * w + lax.broadcasted_iota(jnp.int32, (rows, w), 1)
                hit = jnp.where(col <= cut, (key_scr[kt] == thr).astype(jnp.int32), 0)
                return cnt + _fold_lanes(hit)
            cnt = lax.fori_loop(0, n_tiles, body, jnp.zeros((rows, LANE), jnp.int32))
            return jnp.sum(cnt, axis=1, keepdims=True)

        def bit_step(b, cut):
            cand = cut - jnp.left_shift(jnp.int32(1), 30 - b)
            ok = jnp.logical_and(cand >= 0, count_eq_le(cand) >= need)
            return jnp.where(ok, cand, cut)
        return lax.fori_loop(0, 31, bit_step, jnp.full((rows, 1), 2 ** 31 - 1, jnp.int32))

    excess = jnp.max(jnp.where(n_eq > need, 1, 0)) > 0
    return lax.cond(excess, search, lambda _: jnp.full((rows, 1), 2 ** 31 - 1, jnp.int32), 0)


def _select_bias(key, thr, cut, col, valid):
    bias = jnp.where(key > thr, 0.0, jnp.where(key == thr, jnp.where(col <= cut, 0.0, NEG), NEG))
    return bias if valid is None else jnp.where(valid, bias, NEG)


def _sel_prompt_kernel(iq_ref, ik_ref, iw_ref, bias_ref, key_scr, *, tq, tk, ih, idh, n_sel):
    qi = pl.program_id(0)
    nk_total = bias_ref.shape[1]
    n_tiles = ((qi + 1) * tq + tk - 1) // tk
    row = qi * tq + lax.broadcasted_iota(jnp.int32, (tq, tk), 0)
    lane = lax.broadcasted_iota(jnp.int32, (tq, tk), 1)

    def score_tile(kt, _):
        ik = ik_ref[pl.ds(pl.multiple_of(kt * tk, tk), tk), :]
        acc = jnp.zeros((tq, tk), F32)
        for h in range(ih):
            r = lax.dot_general(iq_ref[:, h * idh:(h + 1) * idh], ik, _NT, preferred_element_type=F32)
            acc = acc + jnp.maximum(r, 0.0) * iw_ref[:, h:h + 1]
        key_scr[kt] = jnp.where(kt * tk + lane <= row, _to_key(acc), INT_MIN)
        return 0
    lax.fori_loop(0, n_tiles, score_tile, 0)

    thr = _kth_largest_key(key_scr, n_tiles, tq, n_sel)
    cut = _tie_cutoff(key_scr, n_tiles, tq, tk, n_sel, thr)

    def emit(kt, _):
        col = kt * tk + lane
        bias_ref[0, kt] = _select_bias(key_scr[kt], thr, cut, col, col <= row)
        return 0
    lax.fori_loop(0, n_tiles, emit, 0)

    def fill(kt, _):
        bias_ref[0, kt] = jnp.full((tq, tk), NEG, F32)
        return 0
    lax.fori_loop(n_tiles, nk_total, fill, 0)


def _select_prompt(iq, ik, iw, t, ih, idh, n_sel, tq, tk):
    nq, nk = t // tq, t // tk
    kern = functools.partial(_sel_prompt_kernel, tq=tq, tk=tk, ih=ih, idh=idh, n_sel=n_sel)
    return pl.pallas_call(
        kern,
        grid=(nq,),
        in_specs=[pl.BlockSpec((tq, ih * idh), lambda i: (i, 0)),
                  pl.BlockSpec((t, idh), lambda i: (0, 0)),
                  pl.BlockSpec((tq, iw.shape[1]), lambda i: (i, 0))],
        out_specs=pl.BlockSpec((1, nk, tq, tk), lambda i: (i, 0, 0, 0)),
        out_shape=jax.ShapeDtypeStruct((nq, nk, tq, tk), F32),
        scratch_shapes=[pltpu.VMEM((nk, tq, tk), jnp.int32)],
        compiler_params=_cparams(("arbitrary",), 48),
    )(iq, ik, iw)


def _attn_prompt_kernel(q_ref, k_ref, v_ref, b_ref, o_ref, m_scr, l_scr, acc_scr, *, tq, tk, nh, dh, scale):
    i, j = pl.program_id(0), pl.program_id(1)
    last = ((i + 1) * tq - 1) // tk

    @pl.when(j == 0)
    def _():
        m_scr[...] = jnp.full(m_scr.shape, -jnp.inf, F32)
        l_scr[...] = jnp.zeros_like(l_scr)
        acc_scr[...] = jnp.zeros_like(acc_scr)

    @pl.when(j <= last)
    def _():
        bias = b_ref[0, 0]
        for h in range(nh):
            sl = slice(h * dh, (h + 1) * dh)
            s = lax.dot_general(q_ref[:, sl], k_ref[:, sl], _NT, preferred_element_type=F32) * scale + bias
            m_prev = m_scr[h]
            m_new = jnp.maximum(m_prev, jnp.max(s, axis=1, keepdims=True))
            alpha = jnp.exp(m_prev - m_new)
            p = jnp.exp(s - m_new)
            l_scr[h] = alpha * l_scr[h] + jnp.sum(p, axis=1, keepdims=True)
            acc_scr[:, sl] = alpha * acc_scr[:, sl] + jnp.dot(p.astype(BF16), v_ref[:, sl],
                                                             preferred_element_type=F32)
            m_scr[h] = m_new

    @pl.when(j == last)
    def _():
        for h in range(nh):
            sl = slice(h * dh, (h + 1) * dh)
            o_ref[:, sl] = (acc_scr[:, sl] / l_scr[h]).astype(o_ref.dtype)


def _attention_prompt(q, k, v, bias, t, nh, dh, tq, tk):
    nq, nk = t // tq, t // tk
    last = lambda i: ((i + 1) * tq - 1) // tk
    kern = functools.partial(_attn_prompt_kernel, tq=tq, tk=tk, nh=nh, dh=dh, scale=dh ** -0.5)
    return pl.pallas_call(
        kern,
        grid=(nq, nk),
        in_specs=[pl.BlockSpec((tq, nh * dh), lambda i, j: (i, 0)),
                  pl.BlockSpec((tk, nh * dh), lambda i, j: (jnp.minimum(j, last(i)), 0)),
                  pl.BlockSpec((tk, nh * dh), lambda i, j: (jnp.minimum(j, last(i)), 0)),
                  pl.BlockSpec((1, 1, tq, tk), lambda i, j: (i, jnp.minimum(j, last(i)), 0, 0))],
        out_specs=pl.BlockSpec((tq, nh * dh), lambda i, j: (i, 0)),
        out_shape=jax.ShapeDtypeStruct((t, nh * dh), BF16),
        scratch_shapes=[pltpu.VMEM((nh, tq, 1), F32), pltpu.VMEM((nh, tq, 1), F32),
                        pltpu.VMEM((tq, nh * dh), F32)],
        compiler_params=_cparams(("arbitrary", "arbitrary"), 40),
    )(q, k, v, bias)


def _sel_sample_kernel(pt_ref, *refs, pp, t, ih, n_sel):
    pages = refs[:pp]
    iq_ref, iw_ref, ikn_ref, bp_ref, bn_ref, key_scr = refs[pp:]
    s = pl.program_id(1)
    n_past = bp_ref.shape[1]
    iq = iq_ref[0]
    iw = iw_ref[0]

    def scores(ik):
        r = lax.dot_general(iq, ik, _NT, preferred_element_type=F32)
        r = jnp.maximum(r, 0.0) * iw
        acc = r[0:t]
        for h in range(1, ih):
            acc = acc + r[h * t:(h + 1) * t]
        return acc

    for p in range(pp):
        key_scr[s * pp + p] = _to_key(scores(pages[p][0].astype(BF16)))

    @pl.when(s == pl.num_programs(1) - 1)
    def _():
        trow = lax.broadcasted_iota(jnp.int32, (t, LANE), 0)
        lane = lax.broadcasted_iota(jnp.int32, (t, LANE), 1)
        new_ok = lane <= trow
        key_scr[n_past] = jnp.where(new_ok, _to_key(scores(ikn_ref[0])), INT_MIN)
        thr = _kth_largest_key(key_scr, n_past + 1, t, n_sel)
        cut = _tie_cutoff(key_scr, n_past + 1, t, LANE, n_sel, thr)

        def emit(kt, _):
            bp_ref[0, kt] = _select_bias(key_scr[kt], thr, cut, kt * LANE + lane, None)
            return 0
        lax.fori_loop(0, n_past, emit, 0)
        bn_ref[0] = _select_bias(key_scr[n_past], thr, cut, n_past * LANE + lane, new_ok)


def _select_sample(page_table, cache_idx_k, iq_rows, iw_rows, ik_new, t, ih, n_sel, pp):
    b, n_pages = page_table.shape
    _, page, idh = cache_idx_k.shape
    assert page == LANE and t <= LANE
    r = ih * t
    page_spec = lambda p: pl.BlockSpec((1, page, idh), lambda i, s, pt: (pt[i, s * pp + p], 0, 0))
    per_b = lambda shp: pl.BlockSpec((1,) + shp, lambda i, s, pt: (i,) + (0,) * len(shp))
    kern = functools.partial(_sel_sample_kernel, pp=pp, t=t, ih=ih, n_sel=n_sel)
    return pl.pallas_call(
        kern,
        grid_spec=pltpu.PrefetchScalarGridSpec(
            num_scalar_prefetch=1,
            grid=(b, n_pages // pp),
            in_specs=[page_spec(p) for p in range(pp)] + [per_b((r, idh)), per_b((r, 1)), per_b((LANE, idh))],
            out_specs=[per_b((n_pages, t, LANE)), per_b((t, LANE))],
            scratch_shapes=[pltpu.VMEM((n_pages + 1, t, LANE), jnp.int32)]),
        out_shape=[jax.ShapeDtypeStruct((b, n_pages, t, LANE), F32),
                   jax.ShapeDtypeStruct((b, t, LANE), F32)],
        compiler_params=_cparams(("arbitrary", "arbitrary"), 32),
    )(page_table, *([cache_idx_k] * pp), iq_rows, iw_rows, ik_new)


def _attn_sample_kernel(pt_ref, *refs, pp, t, nh, dh, scale):
    kp, vp = refs[:pp], refs[pp:2 * pp]
    q_ref, bp_ref, bn_ref, kn_ref, vn_ref, o_ref, m_scr, l_scr, acc_scr = refs[2 * pp:]
    s = pl.program_id(1)
    page = kp[0].shape[0] // nh

    @pl.when(s == 0)
    def _():
        m_scr[...] = jnp.full(m_scr.shape, -jnp.inf, F32)
        l_scr[...] = jnp.zeros_like(l_scr)
        acc_scr[...] = jnp.zeros_like(acc_scr)

    def head_rows(ref, h):
        return ref[pl.ds(h, page, stride=nh), :].astype(BF16)

    def update(k_refs, v_refs, bias):
        cols = []
        for kr in k_refs:
            parts = [lax.dot_general(q_ref[0, h * t:(h + 1) * t, :].astype(BF16), head_rows(kr, h), _NT,
                                     preferred_element_type=F32) for h in range(nh)]
            cols.append(jnp.concatenate(parts, axis=0))
        sc = (cols[0] if len(cols) == 1 else jnp.concatenate(cols, axis=1)) * scale + bias
        m_prev = m_scr[...]
        m_new = jnp.maximum(m_prev, jnp.max(sc, axis=1, keepdims=True))
        alpha = jnp.exp(m_prev - m_new)
        p = jnp.exp(sc - m_new)
        l_scr[...] = alpha * l_scr[...] + jnp.sum(p, axis=1, keepdims=True)
        m_scr[...] = m_new
        pb = p.astype(BF16)
        for h in range(nh):
            rows = slice(h * t, (h + 1) * t)
            o = jnp.zeros((t, dh), F32)
            for i, vr in enumerate(v_refs):
                o = o + jnp.dot(pb[rows, i * page:(i + 1) * page], head_rows(vr, h), preferred_element_type=F32)
            acc_scr[rows, :] = alpha[rows] * acc_scr[rows, :] + o

    bias = jnp.concatenate([jnp.tile(bp_ref[0, p], (nh, 1)) for p in range(pp)], axis=1)
    update(kp, vp, bias)

    @pl.when(s == pl.num_programs(1) - 1)
    def _():
        update([kn_ref.at[0]], [vn_ref.at[0]], jnp.tile(bn_ref[0], (nh, 1)))
        o_ref[0] = (acc_scr[...] / l_scr[...]).astype(o_ref.dtype)


def _attention_sample(page_table, cache_k, cache_v, q_rows, bias_past, bias_new, k_new, v_new, t, nh, dh, pp):
    b, n_pages = page_table.shape
    prow = k_new.shape[1]
    r = nh * t
    page_spec = lambda p: pl.BlockSpec((prow, dh), lambda i, s, pt: (pt[i, s * pp + p], 0))
    per_b = lambda shp: pl.BlockSpec((1,) + shp, lambda i, s, pt: (i,) + (0,) * len(shp))
    kern = functools.partial(_attn_sample_kernel, pp=pp, t=t, nh=nh, dh=dh, scale=dh ** -0.5)
    return pl.pallas_call(
        kern,
        grid_spec=pltpu.PrefetchScalarGridSpec(
            num_scalar_prefetch=1,
            grid=(b, n_pages // pp),
            in_specs=[page_spec(p) for p in range(pp)] * 2 + [
                per_b((r, dh)),
                pl.BlockSpec((1, pp, t, LANE), lambda i, s, pt: (i, s, 0, 0)),
                per_b((t, LANE)), per_b((prow, dh)), per_b((prow, dh))],
            out_specs=per_b((r, dh)),
            scratch_shapes=[pltpu.VMEM((r, 1), F32), pltpu.VMEM((r, 1), F32), pltpu.VMEM((r, dh), F32)]),
        out_shape=jax.ShapeDtypeStruct((b, r, dh), BF16),
        compiler_params=_cparams(("arbitrary", "arbitrary"), 40),
    )(page_table, *([cache_k] * pp), *([cache_v] * pp), q_rows, bias_past, bias_new, k_new, v_new)


def _merge_kernel(ra_ref, wr_ref, at_ref, wa_ref, ga_ref, gb_ref, o_ref):
    a = jnp.dot(ra_ref[...], wr_ref[...].astype(BF16), preferred_element_type=F32)
    b = jnp.dot(at_ref[...], wa_ref[...].astype(BF16), preferred_element_type=F32)
    o_ref[...] = (jax.nn.sigmoid(ga_ref[...]) * a + jax.nn.sigmoid(gb_ref[...]) * b).astype(o_ref.dtype)


def _merge(ra, w_ret_o, att, w_att_o, ga, gb):
    m, kr = ra.shape
    ka = att.shape[1]
    d = w_ret_o.shape[1]
    bm = _pick(m, (528, 512, 256, 128, 64, 32, 16, 8))
    bn = _pick(d, (512, 256, 128))
    row = lambda w: pl.BlockSpec((bm, w), lambda i, j: (i, 0))
    col = lambda kk: pl.BlockSpec((kk, bn), lambda i, j: (0, j))
    tile = pl.BlockSpec((bm, bn), lambda i, j: (i, j))
    return pl.pallas_call(
        _merge_kernel,
        grid=(m // bm, d // bn),
        in_specs=[row(kr), col(kr), row(ka), col(ka), tile, tile],
        out_specs=tile,
        out_shape=jax.ShapeDtypeStruct((m, d), BF16),
        compiler_params=_cparams(("arbitrary", "arbitrary"), 52),
    )(ra, w_ret_o, att, w_att_o, ga, gb)


def _router_kernel(h_ref, w_ref, b_ref, ti_ref, tw_ref, rk_ref, cnt_ref, cnt_scr, *, n_exp):
    s = jax.nn.sigmoid(jnp.dot(h_ref[...].astype(BF16), w_ref[...].astype(BF16), preferred_element_type=F32))
    sb = s + b_ref[...]
    tm = s.shape[0]
    lane = lax.broadcasted_iota(jnp.int32, (tm, n_exp), 1)
    per = n_exp // N_GROUPS
    grp = lane // per
    ninf = -jnp.inf

    def first_argmax(x):
        mx = jnp.max(x, axis=1, keepdims=True)
        idx = jnp.min(jnp.where(x == mx, lane, n_exp), axis=1, keepdims=True)
        return mx, idx

    gscore = []
    for g in range(N_GROUPS):
        xg = jnp.where(grp == g, sb, ninf)
        m1, i1 = first_argmax(xg)
        m2 = jnp.max(jnp.where(lane == i1, ninf, xg), axis=1, keepdims=True)
        gscore.append(m1 + m2)
    keep = jnp.zeros((tm, n_exp), jnp.int32)
    for g in range(N_GROUPS):
        rank = jnp.zeros((tm, 1), jnp.int32)
        for o in range(N_GROUPS):
            if o == g:
                continue
            ahead = (gscore[o] > gscore[g]) if o > g else (gscore[o] >= gscore[g])
            rank = rank + ahead.astype(jnp.int32)
        keep = jnp.where(grp == g, (rank < TOPK_GROUPS).astype(jnp.int32), keep)
    x = jnp.where(keep > 0, sb, ninf)
    olane = lax.broadcasted_iota(jnp.int32, (tm, LANE), 1)
    ti = jnp.zeros((tm, LANE), jnp.int32)
    tw = jnp.zeros((tm, LANE), F32)
    tot = jnp.zeros((tm, 1), F32)
    onehot = jnp.zeros((tm, n_exp), F32)
    picks = []
    for j in range(MOE_TOPK):
        _, idx = first_argmax(x)
        sj = jnp.sum(jnp.where(lane == idx, s, 0.0), axis=1, keepdims=True)
        x = jnp.where(lane == idx, ninf, x)
        onehot = onehot + jnp.where(lane == idx, 1.0, 0.0)
        picks.append(idx)
        ti = jnp.where(olane == j, idx, ti)
        tw = jnp.where(olane == j, sj, tw)
        tot = tot + sj
    ti_ref[...] = ti
    tw_ref[...] = tw / tot * ROUTED_SCALE

    @pl.when(pl.program_id(0) == 0)
    def _():
        cnt_scr[...] = jnp.zeros_like(cnt_scr)
    below = (lax.broadcasted_iota(jnp.int32, (tm, tm), 0) > lax.broadcasted_iota(jnp.int32, (tm, tm), 1))
    ltri = jnp.where(below, 1.0, 0.0).astype(BF16)
    prefix = jnp.dot(ltri, onehot.astype(BF16), preferred_element_type=F32) + cnt_scr[...]
    rk = jnp.zeros((tm, LANE), F32)
    for j in range(MOE_TOPK):
        rj = jnp.sum(jnp.where(lane == picks[j], prefix, 0.0), axis=1, keepdims=True)
        rk = jnp.where(olane == j, rj, rk)
    rk_ref[...] = rk.astype(jnp.int32)
    cnt_scr[...] = cnt_scr[...] + jnp.sum(onehot, axis=0, keepdims=True)
    cnt_ref[...] = cnt_scr[...]


def _route(h, w_router, b_router):
    m, d = h.shape
    n_exp = w_router.shape[1]
    tm = _pick(m, (256, 128, 64, 32, 16, 8))
    return pl.pallas_call(
        functools.partial(_router_kernel, n_exp=n_exp),
        grid=(m // tm,),
        in_specs=[pl.BlockSpec((tm, d), lambda i: (i, 0)),
                  pl.BlockSpec((d, n_exp), lambda i: (0, 0)),
                  pl.BlockSpec((1, n_exp), lambda i: (0, 0))],
        out_specs=[pl.BlockSpec((tm, LANE), lambda i: (i, 0))] * 3 + [pl.BlockSpec((1, n_exp), lambda i: (0, 0))],
        out_shape=[jax.ShapeDtypeStruct((m, LANE), jnp.int32), jax.ShapeDtypeStruct((m, LANE), F32),
                   jax.ShapeDtypeStruct((m, LANE), jnp.int32), jax.ShapeDtypeStruct((1, n_exp), F32)],
        scratch_shapes=[pltpu.VMEM((1, n_exp), F32)],
        compiler_params=_cparams(("arbitrary",), 32),
    )(h, w_router, b_router.reshape(1, n_exp))


def _dest_kernel(ti_ref, rk_ref, base_ref, o_ref, *, n_exp):
    tm = ti_ref.shape[0]
    lane = lax.broadcasted_iota(jnp.int32, (tm, n_exp), 1)
    olane = lax.broadcasted_iota(jnp.int32, (tm, LANE), 1)
    out = jnp.zeros((tm, LANE), jnp.int32)
    for j in range(MOE_TOPK):
        base = jnp.sum(jnp.where(lane == ti_ref[:, j:j + 1], base_ref[...], 0), axis=1, keepdims=True)
        out = jnp.where(olane == j, base + rk_ref[:, j:j + 1], out)
    o_ref[...] = out


def _dest_rows(ti, rk, base):
    m = ti.shape[0]
    n_exp = base.shape[1]
    tm = _pick(m, (256, 128, 64, 32, 16, 8))
    tile = pl.BlockSpec((tm, LANE), lambda i: (i, 0))
    return pl.pallas_call(
        functools.partial(_dest_kernel, n_exp=n_exp),
        grid=(m // tm,),
        in_specs=[tile, tile, pl.BlockSpec((1, n_exp), lambda i: (0, 0))],
        out_specs=tile,
        out_shape=jax.ShapeDtypeStruct((m, LANE), jnp.int32),
        compiler_params=_cparams(("arbitrary",), 32),
    )(ti, rk, base)


def _expert_kernel(be_ref, nu_ref, tok_hbm, h_hbm, wg_ref, wu_ref, wd_ref, y_ref,
                   x_buf, idx_smem, row_sem, idx_sem, *, bm):
    i, hf = pl.program_id(0), pl.program_id(1)
    n_used = nu_ref[0]

    def row_copy(blk_slot, r, tok):
        return pltpu.make_async_copy(h_hbm.at[pl.ds(tok, 1)], x_buf.at[blk_slot, pl.ds(r, 1)],
                                     row_sem.at[blk_slot])

    def start_gather(blk):
        slot = blk % 2
        cp = pltpu.make_async_copy(tok_hbm.at[blk], idx_smem, idx_sem)
        cp.start()
        cp.wait()

        def body(r, _):
            row_copy(slot, r, idx_smem[r]).start()
            return 0
        lax.fori_loop(0, bm, body, 0, unroll=DMA_UNROLL)

    def wait_gather(blk):
        slot = blk % 2
        pltpu.make_async_copy(h_hbm.at[pl.ds(0, bm)], x_buf.at[slot], row_sem.at[slot]).wait()

    @pl.when(jnp.logical_and(i == 0, hf == 0))
    def _():
        start_gather(0)

    @pl.when(jnp.logical_and(i < n_used, hf == 0))
    def _():
        wait_gather(i)

    @pl.when(jnp.logical_and(i + 1 < n_used, hf == 1))
    def _():
        start_gather(i + 1)

    def ffn_half():
        x = x_buf[i % 2].astype(BF16)
        g = jnp.dot(x, wg_ref[0].astype(BF16), preferred_element_type=F32)
        u = jnp.dot(x, wu_ref[0].astype(BF16), preferred_element_type=F32)
        a = (_silu(g) * u).astype(BF16)
        return jnp.dot(a, wd_ref[0].astype(BF16), preferred_element_type=F32)

    @pl.when(jnp.logical_and(i < n_used, hf == 0))
    def _():
        y_ref[...] = ffn_half()

    @pl.when(jnp.logical_and(i < n_used, hf == 1))
    def _():
        y_ref[...] = y_ref[...] + ffn_half()


def _experts(h, block_e, n_used, row_tok, w_gate, w_up, w_down, bm):
    m, d = h.shape
    n_exp, _, de = w_gate.shape
    nb = block_e.shape[0]
    deh = de // 2
    last_used = lambda i, nu: jnp.minimum(i, nu[0] - 1)
    return pl.pallas_call(
        functools.partial(_expert_kernel, bm=bm),
        grid_spec=pltpu.PrefetchScalarGridSpec(
            num_scalar_prefetch=2,
            grid=(nb, 2),
            in_specs=[pl.BlockSpec(memory_space=pl.ANY),
                      pl.BlockSpec(memory_space=pl.ANY),
                      pl.BlockSpec((1, d, deh), lambda i, hf, be, nu: (be[i], 0, hf)),
                      pl.BlockSpec((1, d, deh), lambda i, hf, be, nu: (be[i], 0, hf)),
                      pl.BlockSpec((1, deh, d), lambda i, hf, be, nu: (be[i], hf, 0))],
            out_specs=pl.BlockSpec((bm, d), lambda i, hf, be, nu: (last_used(i, nu), 0)),
            scratch_shapes=[pltpu.VMEM((2, bm, d), F32), pltpu.SMEM((bm,), jnp.int32),
                            pltpu.SemaphoreType.DMA((2,)), pltpu.SemaphoreType.DMA(())]),
        out_shape=jax.ShapeDtypeStruct((nb * bm, d), F32),
        compiler_params=_cparams(("arbitrary", "arbitrary"), 56),
    )(block_e, n_used, row_tok, h, w_gate, w_up, w_down)


def _dispatch(ti, rk, counts, bm):
    m = ti.shape[0]
    n_exp = counts.shape[1]
    n_assign = m * MOE_TOPK
    nb = -(-n_assign // bm) + n_exp
    nblk = (counts[0].astype(jnp.int32) + bm - 1) // bm
    blk_end = jnp.cumsum(nblk)
    blk_start = blk_end - nblk
    n_used = blk_end[-1]
    dest = _dest_rows(ti, rk, (blk_start * bm)[None, :])[:, :MOE_TOPK]
    tok = jnp.arange(n_assign, dtype=jnp.int32) // MOE_TOPK
    row_tok = jnp.zeros((nb * bm,), jnp.int32).at[dest.reshape(-1)].set(tok, unique_indices=True)
    blk = jnp.arange(nb, dtype=jnp.int32)
    be = jnp.minimum(jnp.searchsorted(blk_end, blk, side='right'), n_exp - 1).astype(jnp.int32)
    be = jnp.where(blk < n_used, be, be[jnp.maximum(n_used - 1, 0)])
    return be, n_used.reshape(1).astype(jnp.int32), row_tok.reshape(nb, bm), dest


def _glu_kernel(h_ref, wg_ref, wu_ref, o_ref):
    x = h_ref[...].astype(BF16)
    g = jnp.dot(x, wg_ref[...].astype(BF16), preferred_element_type=F32)
    u = jnp.dot(x, wu_ref[...].astype(BF16), preferred_element_type=F32)
    o_ref[...] = (_silu(g) * u).astype(o_ref.dtype)


def _glu(h, wg, wu):
    m, d = h.shape
    de = wg.shape[1]
    bm = _pick(m, (528, 512, 256, 128, 64, 32, 16, 8))
    bn = _pick(de, (256, 128))
    return pl.pallas_call(
        _glu_kernel,
        grid=(m // bm, de // bn),
        in_specs=[pl.BlockSpec((bm, d), lambda i, j: (i, 0)),
                  pl.BlockSpec((d, bn), lambda i, j: (0, j)),
                  pl.BlockSpec((d, bn), lambda i, j: (0, j))],
        out_specs=pl.BlockSpec((bm, bn), lambda i, j: (i, j)),
        out_shape=jax.ShapeDtypeStruct((m, de), BF16),
        compiler_params=_cparams(("arbitrary", "arbitrary"), 48),
    )(h, wg, wu)


def _combine_kernel(pos_hbm, y_hbm, tw_ref, sh_ref, x_ref, gate_ref, gam_ref, bet_ref, o_ref,
                    buf, idx_smem, row_sem, idx_sem, *, tt):
    b, j = pl.program_id(0), pl.program_id(1)
    step = b * pl.num_programs(1) + j
    n = tt * MOE_TOPK
    cp = pltpu.make_async_copy(pos_hbm.at[step], idx_smem, idx_sem)
    cp.start()
    cp.wait()

    def row_copy(a, src):
        return pltpu.make_async_copy(y_hbm.at[pl.ds(src, 1)], buf.at[a % MOE_TOPK, pl.ds(a // MOE_TOPK, 1)],
                                     row_sem)

    def start(a, _):
        row_copy(a, idx_smem[a]).start()
        return 0
    lax.fori_loop(0, n, start, 0, unroll=DMA_UNROLL)

    for e in range(MOE_TOPK):
        pltpu.make_async_copy(y_hbm.at[pl.ds(0, tt)], buf.at[e], row_sem).wait()

    tw = tw_ref[0]
    routed = buf[0] * tw[:, 0:1]
    for e in range(1, MOE_TOPK):
        routed = routed + buf[e] * tw[:, e:e + 1]
    z = DN_ALPHA * x_ref[0] + gate_ref[0] * (routed + sh_ref[0])
    o_ref[0] = _layer_norm(z) * gam_ref[...] + bet_ref[...]


def _combine(pos, y_sorted, top_w, shared, x1, gate, gamma, beta):
    b, t, d = x1.shape
    tt = _pick(t, (64, 32, 16, 8))
    n = tt * MOE_TOPK
    pos_steps = pos.reshape(-1, n)
    row = lambda w: pl.BlockSpec((1, tt, w), lambda i, j: (i, j, 0))
    mod = pl.BlockSpec((1, 1, d), lambda i, j: (i, 0, 0))
    vec = pl.BlockSpec((1, d), lambda i, j: (0, 0))
    return pl.pallas_call(
        functools.partial(_combine_kernel, tt=tt),
        grid=(b, t // tt),
        in_specs=[pl.BlockSpec(memory_space=pl.ANY), pl.BlockSpec(memory_space=pl.ANY),
                  row(MOE_TOPK), row(d), row(d), mod, vec, vec],
        out_specs=row(d),
        out_shape=jax.ShapeDtypeStruct((b, t, d), F32),
        scratch_shapes=[pltpu.VMEM((MOE_TOPK, tt, d), F32), pltpu.SMEM((n,), jnp.int32),
                        pltpu.SemaphoreType.DMA(()), pltpu.SemaphoreType.DMA(())],
        compiler_params=_cparams(("arbitrary", "arbitrary"), 40),
    )(pos_steps, y_sorted, top_w, shared, x1, gate, gamma.reshape(1, d), beta.reshape(1, d))


def kernel(x_prompt, x_sample, c_prompt, c_sample, cache_k, cache_v, cache_idx_k, state_ret, page_table, w_cond, b_cond, w_in, w_ret_o, w_att_o, w_out, ln_mix_g, ln_mix_b, w_router, b_router, w_exp_gate, w_exp_up, w_exp_down, w_sh_gate, w_sh_up, w_sh_down, ln_ffn_g, ln_ffn_b):
    bp, tp, d = x_prompt.shape
    bs, ts, _ = x_sample.shape
    assert bp == 1, "prompt group kernels are written for a single prompt sequence"
    n_pool, page, ah, dh = cache_k.shape
    idh = cache_idx_k.shape[2]
    rh, dk, dv = state_ret.shape[1:]
    n_pages = page_table.shape[1]
    past = n_pages * page
    n_exp = w_router.shape[1]
    aw, rqk, rv_w = ah * dh, rh * dk, rh * dv
    d_in = w_in.shape[1]
    ih = d_in - (2 * rqk + 2 * rv_w + 3 * aw + idh + 2 * d)
    ih = ih // (idh + 1)
    iq_w = ih * idh
    widths = (rqk, rqk, rv_w, rv_w, aw, aw, aw, iq_w, idh, ih, d, d)
    assert sum(widths) == d_in
    offs = np.concatenate([[0], np.cumsum(widths)]).tolist()
    o_rq, o_rk, o_rv, o_rg, o_aq, o_ak, o_av, o_iq, o_ik, o_iw, o_ga, o_gb = offs[:12]
    mp, ms = bp * tp, bs * ts
    m = mp + ms

    mod = _modulation(jnp.concatenate([c_prompt, c_sample], axis=0), w_cond, b_cond)
    mods_p = [v[:, None, :] for v in jnp.split(mod[:bp], 6, axis=-1)]
    mods_s = [v[:, None, :] for v in jnp.split(mod[bp:], 6, axis=-1)]

    h = jnp.concatenate([_ln_mod(x_prompt, mods_p[1], mods_p[0], BF16).reshape(mp, d),
                         _ln_mod(x_sample, mods_s[1], mods_s[0], BF16).reshape(ms, d)], axis=0)

    pos = jnp.concatenate([jnp.tile(jnp.arange(tp, dtype=jnp.int32), bp),
                           jnp.tile(past + jnp.arange(ts, dtype=jnp.int32), bs)]).astype(F32)
    half = dk // 2
    inv = ROT_BASE ** (-jnp.arange(half, dtype=F32) / half)
    ang = pos[:, None] * inv[None, :]
    cos, sin = jnp.cos(ang), jnp.sin(ang)

    rq, = _proj(h, w_in, o_rq, rqk, [BF16], _ep_rotate(dk, None), extras=(cos, sin))
    rk, = _proj(h, w_in, o_rk, rqk, [BF16], _ep_rotate(dk, dk ** -0.5), extras=(cos, sin))
    rv, = _proj(h, w_in, o_rv, rv_w, [BF16], _ep_store())
    rg, = _proj(h, w_in, o_rg, rv_w, [F32], _ep_store())
    aq, = _proj(h, w_in, o_aq, aw, [BF16], _ep_store())
    ak, ak_b = _proj(h, w_in, o_ak, aw, [F32, BF16], _ep_store())
    av, av_b = _proj(h, w_in, o_av, aw, [F32, BF16], _ep_store())
    iq, = _proj(h, w_in, o_iq, iq_w, [BF16], _ep_store(idh ** -0.5))
    ik, ik_b = _proj(h, w_in, o_ik, idh, [F32, BF16], _ep_store(), bn=LANE)
    iw, = _proj(h, w_in, o_iw, LANE, [F32], _ep_store(ih ** -0.5), bn=LANE)
    w_gates = w_in[:, o_ga:]
    ga, = _proj(h, w_gates, 0, d, [F32], _ep_store())
    gb, = _proj(h, w_gates, d, d, [F32], _ep_store())

    chunk = _pick(tp, (256, 128))
    ra_p, s_p = _retention(rq, rk, rv, rg, jnp.zeros((bp, rh, dk, dv), F32), bp, tp, chunk, chunk, rh, dk, dv)
    pad_s = lambda a: jnp.pad(a[mp:].reshape(bs, ts, -1), ((0, 0), (0, LANE - ts), (0, 0))).reshape(bs * LANE, -1)
    ra_s, s_s = _retention(pad_s(rq), pad_s(rk), pad_s(rv), pad_s(rg), state_ret.astype(F32),
                           bs, LANE, LANE, ts, rh, dk, dv)
    ra_s = ra_s.reshape(bs, LANE, rv_w)[:, :ts]

    tq = _pick(tp, (256, 128))
    tk = _pick(tp, (512, 256, 128))
    bias_p = _select_prompt(iq, ik_b[:mp], iw, tp, ih, idh, min(TOPK_MAX, tp // 4), tq, tk)
    att_p = _attention_prompt(aq, ak_b, av_b, bias_p, tp, ah, dh, tq, tk)

    pp = _pick(n_pages, (4, 2, 1))
    hm = lambda a, nh: a.reshape(bs, ts, nh, -1).transpose(0, 2, 1, 3).reshape(bs, nh * ts, -1)
    iq_rows = hm(iq[mp:], ih)
    iw_rows = iw[mp:, :ih].reshape(bs, ts, ih).transpose(0, 2, 1).reshape(bs, ih * ts, 1)
    pad_new = lambda a: jnp.pad(a[mp:].reshape(bs, ts, -1), ((0, 0), (0, LANE - ts), (0, 0)))
    bias_past, bias_new = _select_sample(page_table, cache_idx_k, iq_rows, iw_rows, pad_new(ik_b),
                                         ts, ih, min(TOPK_MAX, (past + ts) // 4), pp)
    flat_kv = lambda c: c.reshape(n_pool * page * ah, dh)
    new_kv = lambda a: jnp.pad(a[mp:].reshape(bs, ts * ah, dh), ((0, 0), (0, (page - ts) * ah), (0, 0)))
    att_s = _attention_sample(page_table, flat_kv(cache_k), flat_kv(cache_v), hm(aq[mp:], ah).astype(F32),
                              bias_past, bias_new, new_kv(ak), new_kv(av), ts, ah, dh, pp)
    att_s = att_s.reshape(bs, ah, ts, dh).transpose(0, 2, 1, 3)

    ra = jnp.concatenate([ra_p, ra_s.reshape(ms, rv_w)], axis=0)
    att = jnp.concatenate([att_p, att_s.reshape(ms, aw)], axis=0)
    u = _merge(ra, w_ret_o, att, w_att_o, ga, gb)
    mix, = _proj(u, w_out, 0, d, [F32], _ep_store())
    x1_p, h2_p = _post_ln(x_prompt, mix.reshape(1, m, d), mods_p[2], ln_mix_g, ln_mix_b, mods_p[4], mods_p[3])
    x1_s, h2_s = _post_ln(x_sample, mix[mp:].reshape(bs, ts, d), mods_s[2], ln_mix_g, ln_mix_b, mods_s[4], mods_s[3])
    h2 = jnp.concatenate([h2_p.reshape(mp, d), h2_s.reshape(ms, d)], axis=0)

    top_i, top_w, rank, counts = _route(h2, w_router, b_router)
    top_w = top_w[:, :MOE_TOPK]
    n_assign = m * MOE_TOPK
    bm = 320 if n_assign // n_exp >= 256 else _pick(max(n_assign // n_exp, 8), (128, 64, 32, 16, 8))
    block_e, n_used, row_tok, dest = _dispatch(top_i, rank, counts, bm)
    y_sorted = _experts(h2, block_e, n_used, row_tok, w_exp_gate, w_exp_up, w_exp_down, bm)
    shared, = _proj(_glu(h2, w_sh_gate, w_sh_up), w_sh_down, 0, d, [F32], _ep_store())

    y_p = _combine(dest[:mp], y_sorted, top_w[:mp].reshape(bp, tp, MOE_TOPK), shared.reshape(1, m, d),
                   x1_p, mods_p[5], ln_ffn_g, ln_ffn_b)
    y_s = _combine(dest[mp:], y_sorted, top_w[mp:].reshape(bs, ts, MOE_TOPK), shared[mp:].reshape(bs, ts, d),
                   x1_s, mods_s[5], ln_ffn_g, ln_ffn_b)

    return (y_p, y_s,
            ak[:mp].reshape(bp, tp, ah, dh), av[:mp].reshape(bp, tp, ah, dh), ik[:mp].reshape(bp, tp, idh),
            s_p.reshape(bp, rh, dk, dv),
            ak[mp:].reshape(bs, ts, ah, dh), av[mp:].reshape(bs, ts, ah, dh), ik[mp:].reshape(bs, ts, idh),
            s_s)
```

```python
import functools
import math

import jax
import jax.numpy as jnp
import numpy as np
from jax import lax
from jax.experimental import pallas as pl
from jax.experimental.pallas import tpu as pltpu

ROT_BASE = 10000.0
TOPK_MAX = 256
Q_BLOCK = 128
MOE_TOPK = 8
N_GROUPS = 8
TOPK_GROUPS = 4
ROUTED_SCALE = 2.5
LN_EPS = 1e-5
DEPTH = 1
DN_ALPHA = (2.0 * DEPTH) ** 0.25

LANE = 128
SUBLANE = 8
VMEM_BYTES_V7X = 64 * 2 ** 20

DMA_UNROLL = 8

NEG = -1e30
INT_MIN = -2 ** 31
BF16 = jnp.bfloat16
F32 = jnp.float32

_NT = (((1,), (1,)), ((), ()))
_TN = (((0,), (0,)), ((), ()))


def _cparams(sem, vmem_mb):
    assert vmem_mb * 2 ** 20 < VMEM_BYTES_V7X
    return pltpu.CompilerParams(dimension_semantics=sem, vmem_limit_bytes=vmem_mb * 2 ** 20)


def _pick(n, cands):
    for c in cands:
        if c <= n and n % c == 0:
            return c
    return n


def _silu(x):
    return x * jax.nn.sigmoid(x)


def _layer_norm(x):
    mu = jnp.mean(x, axis=-1, keepdims=True)
    xc = x - mu
    var = jnp.mean(xc * xc, axis=-1, keepdims=True)
    return xc * lax.rsqrt(var + LN_EPS)


def _cond_kernel(c_ref, w_ref, b_ref, o_ref):
    a = _silu(c_ref[...]).astype(BF16)
    o_ref[...] = jnp.dot(a, w_ref[...].astype(BF16), preferred_element_type=F32) + b_ref[...]


def _modulation(c, w_cond, b_cond):
    nb, d = c.shape
    n = w_cond.shape[1]
    mp = -(-nb // SUBLANE) * SUBLANE
    cp = jnp.pad(c, ((0, mp - nb), (0, 0)))
    tn = _pick(n, (512, 256, 128))
    out = pl.pallas_call(
        _cond_kernel,
        grid=(n // tn,),
        in_specs=[pl.BlockSpec((mp, d), lambda j: (0, 0)),
                  pl.BlockSpec((d, tn), lambda j: (0, j)),
                  pl.BlockSpec((1, tn), lambda j: (0, j))],
        out_specs=pl.BlockSpec((mp, tn), lambda j: (0, j)),
        out_shape=jax.ShapeDtypeStruct((mp, n), F32),
        compiler_params=_cparams(("arbitrary",), 40),
    )(cp, w_cond, b_cond.reshape(1, n))
    return out[:nb]


def _ln_mod_kernel(x_ref, sc_ref, sh_ref, o_ref):
    y = _layer_norm(x_ref[0])
    o_ref[0] = (y * (1.0 + sc_ref[0]) + sh_ref[0]).astype(o_ref.dtype)


def _ln_mod(x, scale, shift, out_dtype):
    b, t, d = x.shape
    tt = _pick(t, (256, 128, 64, 32, 16, 8))
    mod = pl.BlockSpec((1, 1, d), lambda i, j: (i, 0, 0))
    row = pl.BlockSpec((1, tt, d), lambda i, j: (i, j, 0))
    return pl.pallas_call(
        _ln_mod_kernel,
        grid=(b, t // tt),
        in_specs=[row, mod, mod],
        out_specs=row,
        out_shape=jax.ShapeDtypeStruct((b, t, d), out_dtype),
        compiler_params=_cparams(("arbitrary", "arbitrary"), 40),
    )(x, scale, shift)


def _post_ln_kernel(x_ref, y_ref, gate_ref, gam_ref, bet_ref, sc_ref, sh_ref, x1_ref, h_ref):
    z = DN_ALPHA * x_ref[0] + gate_ref[0] * y_ref[0]
    x1 = _layer_norm(z) * gam_ref[...] + bet_ref[...]
    x1_ref[0] = x1
    h_ref[0] = (_layer_norm(x1) * (1.0 + sc_ref[0]) + sh_ref[0]).astype(h_ref.dtype)


def _post_ln(x, y, gate, gamma, beta, scale, shift):
    b, t, d = x.shape
    tt = _pick(t, (256, 128, 64, 32, 16, 8))
    mod = pl.BlockSpec((1, 1, d), lambda i, j: (i, 0, 0))
    row = pl.BlockSpec((1, tt, d), lambda i, j: (i, j, 0))
    vec = pl.BlockSpec((1, d), lambda i, j: (0, 0))
    return pl.pallas_call(
        _post_ln_kernel,
        grid=(b, t // tt),
        in_specs=[row, row, mod, vec, vec, mod, mod],
        out_specs=[row, row],
        out_shape=[jax.ShapeDtypeStruct((b, t, d), F32), jax.ShapeDtypeStruct((b, t, d), F32)],
        compiler_params=_cparams(("arbitrary", "arbitrary"), 48),
    )(x, y, gate, gamma.reshape(1, d), beta.reshape(1, d), scale, shift)


def _proj_kernel(*refs, n_extra, n_out, epilogue):
    a_ref, w_ref = refs[0], refs[1]
    extra = refs[2:2 + n_extra]
    outs = refs[2 + n_extra:2 + n_extra + n_out]
    acc = jnp.dot(a_ref[...], w_ref[...].astype(BF16), preferred_element_type=F32)
    epilogue(acc, extra, outs)


def _proj(a, w, col0, ncols, out_dtypes, epilogue, extras=(), bm=None, bn=None, vmem_mb=52):
    m, k = a.shape
    bm = bm or _pick(m, (1056, 1024, 768, 512, 256, 128, 64, 32, 16, 8))
    bn = bn or _pick(math.gcd(ncols, col0) if col0 else ncols, (512, 256, 128))
    assert col0 % bn == 0 and ncols % bn == 0 and m % bm == 0
    cb = col0 // bn
    in_specs = [pl.BlockSpec((bm, k), lambda i, j: (i, 0)),
                pl.BlockSpec((k, bn), lambda i, j: (0, cb + j))]
    for e in extras:
        in_specs.append(pl.BlockSpec((bm, e.shape[1]), lambda i, j: (i, 0)))
    out_specs = [pl.BlockSpec((bm, bn), lambda i, j: (i, j)) for _ in out_dtypes]
    out_shape = [jax.ShapeDtypeStruct((m, ncols), dt) for dt in out_dtypes]
    kern = functools.partial(_proj_kernel, n_extra=len(extras), n_out=len(out_dtypes), epilogue=epilogue)
    return pl.pallas_call(
        kern,
        grid=(m // bm, ncols // bn),
        in_specs=in_specs,
        out_specs=out_specs,
        out_shape=out_shape,
        compiler_params=_cparams(("arbitrary", "arbitrary"), vmem_mb),
    )(a, w, *extras)


def _ep_store(scale=None):
    def ep(acc, extra, outs):
        v = acc if scale is None else acc * scale
        for o in outs:
            o[...] = v.astype(o.dtype)
    return ep


def _ep_rotate(dk, scale):
    half = dk // 2

    def ep(acc, extra, outs):
        cos, sin = extra[0][...], extra[1][...]
        (o,) = outs
        for h in range(acc.shape[1] // dk):
            x1 = acc[:, h * dk:h * dk + half]
            x2 = acc[:, h * dk + half:(h + 1) * dk]
            r1 = x1 * cos - x2 * sin
            r2 = x2 * cos + x1 * sin
            if scale is not None:
                r1, r2 = r1 * scale, r2 * scale
            o[:, h * dk:h * dk + half] = r1.astype(o.dtype)
            o[:, h * dk + half:(h + 1) * dk] = r2.astype(o.dtype)
    return ep


def _ret_kernel(q_ref, k_ref, v_ref, g_ref, s0_ref, intra_ref, qd_ref, kd_ref, cd_ref,
                o_ref, s_out_ref, s_scr):
    c = pl.program_id(2)

    @pl.when(c == 0)
    def _():
        s_scr[...] = s0_ref[0, 0]

    q, k, v = q_ref[...], k_ref[...], v_ref[...]
    s = s_scr[...]
    att = lax.dot_general(q, k, _NT, preferred_element_type=F32) * intra_ref[0]
    o = jnp.dot(att.astype(BF16), v, preferred_element_type=F32)
    o = o + jnp.dot((q.astype(F32) * qd_ref[0]).astype(BF16), s.astype(BF16), preferred_element_type=F32)
    kd = (k.astype(F32) * kd_ref[0]).astype(BF16)
    s_new = s * cd_ref[0] + lax.dot_general(kd, v, _TN, preferred_element_type=F32)
    s_scr[...] = s_new
    o_ref[...] = (_silu(g_ref[...]) * _layer_norm(o)).astype(o_ref.dtype)

    @pl.when(c == pl.num_programs(2) - 1)
    def _():
        s_out_ref[0, 0] = s_new


def _ret_tables(nh, chunk, n_valid, dv):
    lg = jnp.log1p(-jnp.exp2(-5.0 - jnp.arange(nh, dtype=F32)))
    idx = jnp.arange(chunk, dtype=F32)
    diff = idx[:, None] - idx[None, :]
    intra = jnp.where(diff >= 0, jnp.exp(lg[:, None, None] * jnp.maximum(diff, 0.0)), 0.0)
    q_dec = jnp.exp(lg[:, None] * (idx[None, :] + 1.0))[:, :, None]
    k_dec = jnp.where(idx[None, :] < n_valid, jnp.exp(lg[:, None] * jnp.maximum(n_valid - 1.0 - idx[None, :], 0.0)),
                      0.0)[:, :, None]
    c_dec = jnp.broadcast_to(jnp.exp(lg * n_valid)[:, None, None], (nh, 1, dv))
    return intra, q_dec, k_dec, c_dec


def _retention(q, k, v, g, s0, nb, t, chunk, n_valid, nh, dk, dv):
    nc = t // chunk
    intra, q_dec, k_dec, c_dec = _ret_tables(nh, chunk, n_valid, dv)
    tab = lambda shp: pl.BlockSpec((1,) + shp, lambda b, h, c: (h, 0, 0))
    row = lambda w: pl.BlockSpec((chunk, w), lambda b, h, c: (b * nc + c, h))
    st = pl.BlockSpec((1, 1, dk, dv), lambda b, h, c: (b, h, 0, 0))
    return pl.pallas_call(
        _ret_kernel,
        grid=(nb, nh, nc),
        in_specs=[row(dk), row(dk), row(dv), row(dv), st,
                  tab((chunk, chunk)), tab((chunk, 1)), tab((chunk, 1)), tab((1, dv))],
        out_specs=[row(dv), st],
        out_shape=[jax.ShapeDtypeStruct((nb * t, nh * dv), BF16),
                   jax.ShapeDtypeStruct((nb, nh, dk, dv), F32)],
        scratch_shapes=[pltpu.VMEM((dk, dv), F32)],
        compiler_params=_cparams(("arbitrary", "arbitrary", "arbitrary"), 32),
    )(q, k, v, g, s0, intra, q_dec, k_dec, c_dec)


def _to_key(score):
    bits = pltpu.bitcast(score + 0.0, jnp.int32)
    return jnp.where(bits >= 0, bits, bits ^ jnp.int32(0x7FFFFFFF))


def _fold_lanes(x):
    acc = x[:, :LANE]
    for g in range(1, x.shape[1] // LANE):
        acc = acc + x[:, g * LANE:(g + 1) * LANE]
    return acc


def _kth_largest_key(key_scr, n_tiles, rows, n_sel):
    def count_ge(cand):
        def body(kt, cnt):
            return cnt + _fold_lanes((key_scr[kt] >= cand).astype(jnp.int32))
        cnt = lax.fori_loop(0, n_tiles, body, jnp.zeros((rows, LANE), jnp.int32))
        return jnp.sum(cnt, axis=1, keepdims=True)

    def bit_step(b, t):
        cand = t + jnp.left_shift(jnp.int32(1), 31 - b)
        return jnp.where(count_ge(cand) >= n_sel, cand, t)

    return lax.fori_loop(0, 32, bit_step, jnp.full((rows, 1), INT_MIN, jnp.int32))


def _tie_cutoff(key_scr, n_tiles, rows, w, n_sel, thr):
    def counts(kt, c):
        k = key_scr[kt]
        gt = c[0] + _fold_lanes((k > thr).astype(jnp.int32))
        eq = c[1] + _fold_lanes((k == thr).astype(jnp.int32))
        return gt, eq
    z = jnp.zeros((rows, LANE), jnp.int32)
    gt, eq = lax.fori_loop(0, n_tiles, counts, (z, z))
    need = n_sel - jnp.sum(gt, axis=1, keepdims=True)
    n_eq = jnp.sum(eq, axis=1, keepdims=True)

    def search(_):
        def count_eq_le(cut):
            def body(kt, cnt):
                col = kt * w + lax.broadcasted_iota(jnp.int32, (rows, w), 1)
                hit = jnp.where(col <= cut, (key_scr[kt] == thr).astype(jnp.int32), 0)
                return cnt + _fold_lanes(hit)
            cnt = lax.fori_loop(0, n_tiles, body, jnp.zeros((rows, LANE), jnp.int32))
            return jnp.sum(cnt, axis=1, keepdims=True)

        def bit_step(b, cut):
            cand = cut - jnp.left_shift(jnp.int32(1), 30 - b)
            ok = jnp.logical_and(cand >= 0, count_eq_le(cand) >= need)
            return jnp.where(ok, cand, cut)
        return lax.fori_loop(0, 31, bit_step, jnp.full((rows, 1), 2 ** 31 - 1, jnp.int32))

    excess = jnp.max(jnp.where(n_eq > need, 1, 0)) > 0
    return lax.cond(excess, search, lambda _: jnp.full((rows, 1), 2 ** 31 - 1, jnp.int32), 0)


def _select_bias(key, thr, cut, col, valid):
    bias = jnp.where(key > thr, 0.0, jnp.where(key == thr, jnp.where(col <= cut, 0.0, NEG), NEG))
    return bias if valid is None else jnp.where(valid, bias, NEG)


def _sel_prompt_kernel(iq_ref, ik_ref, iw_ref, bias_ref, key_scr, *, tq, tk, ih, idh, n_sel):
    qi = pl.program_id(0)
    nk_total = bias_ref.shape[1]
    n_tiles = ((qi + 1) * tq + tk - 1) // tk
    row = qi * tq + lax.broadcasted_iota(jnp.int32, (tq, tk), 0)
    lane = lax.broadcasted_iota(jnp.int32, (tq, tk), 1)

    def score_tile(kt, _):
        ik = ik_ref[pl.ds(pl.multiple_of(kt * tk, tk), tk), :]
        acc = jnp.zeros((tq, tk), F32)
        for h in range(ih):
            r = lax.dot_general(iq_ref[:, h * idh:(h + 1) * idh], ik, _NT, preferred_element_type=F32)
            acc = acc + jnp.maximum(r, 0.0) * iw_ref[:, h:h + 1]
        key_scr[kt] = jnp.where(kt * tk + lane <= row, _to_key(acc), INT_MIN)
        return 0
    lax.fori_loop(0, n_tiles, score_tile, 0)

    thr = _kth_largest_key(key_scr, n_tiles, tq, n_sel)
    cut = _tie_cutoff(key_scr, n_tiles, tq, tk, n_sel, thr)

    def emit(kt, _):
        col = kt * tk + lane
        bias_ref[0, kt] = _select_bias(key_scr[kt], thr, cut, col, col <= row)
        return 0
    lax.fori_loop(0, n_tiles, emit, 0)

    def fill(kt, _):
        bias_ref[0, kt] = jnp.full((tq, tk), NEG, F32)
        return 0
    lax.fori_loop(n_tiles, nk_total, fill, 0)


def _select_prompt(iq, ik, iw, t, ih, idh, n_sel, tq, tk):
    nq, nk = t // tq, t // tk
    kern = functools.partial(_sel_prompt_kernel, tq=tq, tk=tk, ih=ih, idh=idh, n_sel=n_sel)
    return pl.pallas_call(
        kern,
        grid=(nq,),
        in_specs=[pl.BlockSpec((tq, ih * idh), lambda i: (i, 0)),
                  pl.BlockSpec((t, idh), lambda i: (0, 0)),
                  pl.BlockSpec((tq, iw.shape[1]), lambda i: (i, 0))],
        out_specs=pl.BlockSpec((1, nk, tq, tk), lambda i: (i, 0, 0, 0)),
        out_shape=jax.ShapeDtypeStruct((nq, nk, tq, tk), F32),
        scratch_shapes=[pltpu.VMEM((nk, tq, tk), jnp.int32)],
        compiler_params=_cparams(("arbitrary",), 48),
    )(iq, ik, iw)


def _attn_prompt_kernel(q_ref, k_ref, v_ref, b_ref, o_ref, m_scr, l_scr, acc_scr, *, tq, tk, nh, dh, scale):
    i, j = pl.program_id(0), pl.program_id(1)
    last = ((i + 1) * tq - 1) // tk

    @pl.when(j == 0)
    def _():
        m_scr[...] = jnp.full(m_scr.shape, -jnp.inf, F32)
        l_scr[...] = jnp.zeros_like(l_scr)
        acc_scr[...] = jnp.zeros_like(acc_scr)

    @pl.when(j <= last)
    def _():
        nsub = b_ref.shape[0]
        bias = b_ref[0, 0] if nsub == 1 else jnp.concatenate([b_ref[a, 0] for a in range(nsub)], axis=0)
        for h in range(nh):
            sl = slice(h * dh, (h + 1) * dh)
            s = lax.dot_general(q_ref[:, sl], k_ref[:, sl], _NT, preferred_element_type=F32) * scale + bias
            m_prev = m_scr[h]
            m_new = jnp.maximum(m_prev, jnp.max(s, axis=1, keepdims=True))
            alpha = jnp.exp(m_prev - m_new)
            p = jnp.exp(s - m_new)
            l_scr[h] = alpha * l_scr[h] + jnp.sum(p, axis=1, keepdims=True)
            acc_scr[:, sl] = alpha * acc_scr[:, sl] + jnp.dot(p.astype(BF16), v_ref[:, sl],
                                                             preferred_element_type=F32)
            m_scr[h] = m_new

    @pl.when(j == last)
    def _():
        for h in range(nh):
            sl = slice(h * dh, (h + 1) * dh)
            o_ref[:, sl] = (acc_scr[:, sl] / l_scr[h]).astype(o_ref.dtype)


def _attention_prompt(q, k, v, bias, t, nh, dh, tq, tk):
    nq, nk = t // tq, t // tk
    nsub = tq // bias.shape[2]
    last = lambda i: ((i + 1) * tq - 1) // tk
    kern = functools.partial(_attn_prompt_kernel, tq=tq, tk=tk, nh=nh, dh=dh, scale=dh ** -0.5)
    return pl.pallas_call(
        kern,
        grid=(nq, nk),
        in_specs=[pl.BlockSpec((tq, nh * dh), lambda i, j: (i, 0)),
                  pl.BlockSpec((tk, nh * dh), lambda i, j: (jnp.minimum(j, last(i)), 0)),
                  pl.BlockSpec((tk, nh * dh), lambda i, j: (jnp.minimum(j, last(i)), 0)),
                  pl.BlockSpec((nsub, 1, tq // nsub, tk), lambda i, j: (i, jnp.minimum(j, last(i)), 0, 0))],
        out_specs=pl.BlockSpec((tq, nh * dh), lambda i, j: (i, 0)),
        out_shape=jax.ShapeDtypeStruct((t, nh * dh), BF16),
        scratch_shapes=[pltpu.VMEM((nh, tq, 1), F32), pltpu.VMEM((nh, tq, 1), F32),
                        pltpu.VMEM((tq, nh * dh), F32)],
        compiler_params=_cparams(("arbitrary", "arbitrary"), 48),
    )(q, k, v, bias)


def _sel_sample_kernel(pt_ref, *refs, pp, t, ih, n_sel):
    pages = refs[:pp]
    iq_ref, iw_ref, ikn_ref, bp_ref, bn_ref, key_scr = refs[pp:]
    s = pl.program_id(1)
    n_past = bp_ref.shape[1]
    iq = iq_ref[0]
    iw = iw_ref[0]

    def scores(ik):
        r = lax.dot_general(iq, ik, _NT, preferred_element_type=F32)
        r = jnp.maximum(r, 0.0) * iw
        acc = r[0:t]
        for h in range(1, ih):
            acc = acc + r[h * t:(h + 1) * t]
        return acc

    for p in range(pp):
        key_scr[s * pp + p] = _to_key(scores(pages[p][0].astype(BF16)))

    @pl.when(s == pl.num_programs(1) - 1)
    def _():
        trow = lax.broadcasted_iota(jnp.int32, (t, LANE), 0)
        lane = lax.broadcasted_iota(jnp.int32, (t, LANE), 1)
        new_ok = lane <= trow
        key_scr[n_past] = jnp.where(new_ok, _to_key(scores(ikn_ref[0])), INT_MIN)
        thr = _kth_largest_key(key_scr, n_past + 1, t, n_sel)
        cut = _tie_cutoff(key_scr, n_past + 1, t, LANE, n_sel, thr)

        def emit(kt, _):
            bp_ref[0, kt] = _select_bias(key_scr[kt], thr, cut, kt * LANE + lane, None)
            return 0
        lax.fori_loop(0, n_past, emit, 0)
        bn_ref[0] = _select_bias(key_scr[n_past], thr, cut, n_past * LANE + lane, new_ok)


def _select_sample(page_table, cache_idx_k, iq_rows, iw_rows, ik_new, t, ih, n_sel, pp):
    b, n_pages = page_table.shape
    _, page, idh = cache_idx_k.shape
    assert page == LANE and t <= LANE
    r = ih * t
    page_spec = lambda p: pl.BlockSpec((1, page, idh), lambda i, s, pt: (pt[i, s * pp + p], 0, 0))
    per_b = lambda shp: pl.BlockSpec((1,) + shp, lambda i, s, pt: (i,) + (0,) * len(shp))
    kern = functools.partial(_sel_sample_kernel, pp=pp, t=t, ih=ih, n_sel=n_sel)
    return pl.pallas_call(
        kern,
        grid_spec=pltpu.PrefetchScalarGridSpec(
            num_scalar_prefetch=1,
            grid=(b, n_pages // pp),
            in_specs=[page_spec(p) for p in range(pp)] + [per_b((r, idh)), per_b((r, 1)), per_b((LANE, idh))],
            out_specs=[per_b((n_pages, t, LANE)), per_b((t, LANE))],
            scratch_shapes=[pltpu.VMEM((n_pages + 1, t, LANE), jnp.int32)]),
        out_shape=[jax.ShapeDtypeStruct((b, n_pages, t, LANE), F32),
                   jax.ShapeDtypeStruct((b, t, LANE), F32)],
        compiler_params=_cparams(("arbitrary", "arbitrary"), 32),
    )(page_table, *([cache_idx_k] * pp), iq_rows, iw_rows, ik_new)


def _attn_sample_kernel(pt_ref, *refs, pp, t, nh, dh, scale):
    kp, vp = refs[:pp], refs[pp:2 * pp]
    q_ref, bp_ref, bn_ref, kn_ref, vn_ref, o_ref, m_scr, l_scr, acc_scr = refs[2 * pp:]
    s = pl.program_id(1)
    page = kp[0].shape[0] // nh

    @pl.when(s == 0)
    def _():
        m_scr[...] = jnp.full(m_scr.shape, -jnp.inf, F32)
        l_scr[...] = jnp.zeros_like(l_scr)
        acc_scr[...] = jnp.zeros_like(acc_scr)

    def head_rows(ref, h):
        return ref[pl.ds(h, page, stride=nh), :].astype(BF16)

    def update(k_refs, v_refs, bias):
        cols = []
        for kr in k_refs:
            parts = [lax.dot_general(q_ref[0, h * t:(h + 1) * t, :].astype(BF16), head_rows(kr, h), _NT,
                                     preferred_element_type=F32) for h in range(nh)]
            cols.append(jnp.concatenate(parts, axis=0))
        sc = (cols[0] if len(cols) == 1 else jnp.concatenate(cols, axis=1)) * scale + bias
        m_prev = m_scr[...]
        m_new = jnp.maximum(m_prev, jnp.max(sc, axis=1, keepdims=True))
        alpha = jnp.exp(m_prev - m_new)
        p = jnp.exp(sc - m_new)
        l_scr[...] = alpha * l_scr[...] + jnp.sum(p, axis=1, keepdims=True)
        m_scr[...] = m_new
        pb = p.astype(BF16)
        for h in range(nh):
            rows = slice(h * t, (h + 1) * t)
            o = jnp.zeros((t, dh), F32)
            for i, vr in enumerate(v_refs):
                o = o + jnp.dot(pb[rows, i * page:(i + 1) * page], head_rows(vr, h), preferred_element_type=F32)
            acc_scr[rows, :] = alpha[rows] * acc_scr[rows, :] + o

    bias = jnp.concatenate([jnp.tile(bp_ref[0, p], (nh, 1)) for p in range(pp)], axis=1)
    update(kp, vp, bias)

    @pl.when(s == pl.num_programs(1) - 1)
    def _():
        update([kn_ref.at[0]], [vn_ref.at[0]], jnp.tile(bn_ref[0], (nh, 1)))
        o_ref[0] = (acc_scr[...] / l_scr[...]).astype(o_ref.dtype)


def _attention_sample(page_table, cache_k, cache_v, q_rows, bias_past, bias_new, k_new, v_new, t, nh, dh, pp):
    b, n_pages = page_table.shape
    prow = k_new.shape[1]
    r = nh * t
    page_spec = lambda p: pl.BlockSpec((prow, dh), lambda i, s, pt: (pt[i, s * pp + p], 0))
    per_b = lambda shp: pl.BlockSpec((1,) + shp, lambda i, s, pt: (i,) + (0,) * len(shp))
    kern = functools.partial(_attn_sample_kernel, pp=pp, t=t, nh=nh, dh=dh, scale=dh ** -0.5)
    return pl.pallas_call(
        kern,
        grid_spec=pltpu.PrefetchScalarGridSpec(
            num_scalar_prefetch=1,
            grid=(b, n_pages // pp),
            in_specs=[page_spec(p) for p in range(pp)] * 2 + [
                per_b((r, dh)),
                pl.BlockSpec((1, pp, t, LANE), lambda i, s, pt: (i, s, 0, 0)),
                per_b((t, LANE)), per_b((prow, dh)), per_b((prow, dh))],
            out_specs=per_b((r, dh)),
            scratch_shapes=[pltpu.VMEM((r, 1), F32), pltpu.VMEM((r, 1), F32), pltpu.VMEM((r, dh), F32)]),
        out_shape=jax.ShapeDtypeStruct((b, r, dh), BF16),
        compiler_params=_cparams(("arbitrary", "arbitrary"), 40),
    )(page_table, *([cache_k] * pp), *([cache_v] * pp), q_rows, bias_past, bias_new, k_new, v_new)


def _merge_kernel(ra_ref, wr_ref, at_ref, wa_ref, ga_ref, gb_ref, o_ref):
    a = jnp.dot(ra_ref[...], wr_ref[...].astype(BF16), preferred_element_type=F32)
    b = jnp.dot(at_ref[...], wa_ref[...].astype(BF16), preferred_element_type=F32)
    o_ref[...] = (jax.nn.sigmoid(ga_ref[...]) * a + jax.nn.sigmoid(gb_ref[...]) * b).astype(o_ref.dtype)


def _merge(ra, w_ret_o, att, w_att_o, ga, gb):
    m, kr = ra.shape
    ka = att.shape[1]
    d = w_ret_o.shape[1]
    bm = _pick(m, (528, 512, 256, 128, 64, 32, 16, 8))
    bn = _pick(d, (512, 256, 128))
    row = lambda w: pl.BlockSpec((bm, w), lambda i, j: (i, 0))
    col = lambda kk: pl.BlockSpec((kk, bn), lambda i, j: (0, j))
    tile = pl.BlockSpec((bm, bn), lambda i, j: (i, j))
    return pl.pallas_call(
        _merge_kernel,
        grid=(m // bm, d // bn),
        in_specs=[row(kr), col(kr), row(ka), col(ka), tile, tile],
        out_specs=tile,
        out_shape=jax.ShapeDtypeStruct((m, d), BF16),
        compiler_params=_cparams(("arbitrary", "arbitrary"), 52),
    )(ra, w_ret_o, att, w_att_o, ga, gb)


def _router_kernel(h_ref, w_ref, b_ref, ti_ref, tw_ref, rk_ref, cnt_ref, cnt_scr, *, n_exp):
    s = jax.nn.sigmoid(jnp.dot(h_ref[...].astype(BF16), w_ref[...].astype(BF16), preferred_element_type=F32))
    sb = s + b_ref[...]
    tm = s.shape[0]
    lane = lax.broadcasted_iota(jnp.int32, (tm, n_exp), 1)
    per = n_exp // N_GROUPS
    grp = lane // per
    ninf = -jnp.inf

    def first_argmax(x):
        mx = jnp.max(x, axis=1, keepdims=True)
        idx = jnp.min(jnp.where(x == mx, lane, n_exp), axis=1, keepdims=True)
        return mx, idx

    gscore = []
    for g in range(N_GROUPS):
        xg = jnp.where(grp == g, sb, ninf)
        m1, i1 = first_argmax(xg)
        m2 = jnp.max(jnp.where(lane == i1, ninf, xg), axis=1, keepdims=True)
        gscore.append(m1 + m2)
    keep = jnp.zeros((tm, n_exp), jnp.int32)
    for g in range(N_GROUPS):
        rank = jnp.zeros((tm, 1), jnp.int32)
        for o in range(N_GROUPS):
            if o == g:
                continue
            ahead = (gscore[o] > gscore[g]) if o > g else (gscore[o] >= gscore[g])
            rank = rank + ahead.astype(jnp.int32)
        keep = jnp.where(grp == g, (rank < TOPK_GROUPS).astype(jnp.int32), keep)
    x = jnp.where(keep > 0, sb, ninf)
    olane = lax.broadcasted_iota(jnp.int32, (tm, LANE), 1)
    ti = jnp.zeros((tm, LANE), jnp.int32)
    tw = jnp.zeros((tm, LANE), F32)
    tot = jnp.zeros((tm, 1), F32)
    onehot = jnp.zeros((tm, n_exp), F32)
    picks = []
    for j in range(MOE_TOPK):
        _, idx = first_argmax(x)
        sj = jnp.sum(jnp.where(lane == idx, s, 0.0), axis=1, keepdims=True)
        x = jnp.where(lane == idx, ninf, x)
        onehot = onehot + jnp.where(lane == idx, 1.0, 0.0)
        picks.append(idx)
        ti = jnp.where(olane == j, idx, ti)
        tw = jnp.where(olane == j, sj, tw)
        tot = tot + sj
    ti_ref[...] = ti
    tw_ref[...] = tw / tot * ROUTED_SCALE

    @pl.when(pl.program_id(0) == 0)
    def _():
        cnt_scr[...] = jnp.zeros_like(cnt_scr)
    below = (lax.broadcasted_iota(jnp.int32, (tm, tm), 0) > lax.broadcasted_iota(jnp.int32, (tm, tm), 1))
    ltri = jnp.where(below, 1.0, 0.0).astype(BF16)
    prefix = jnp.dot(ltri, onehot.astype(BF16), preferred_element_type=F32) + cnt_scr[...]
    rk = jnp.zeros((tm, LANE), F32)
    for j in range(MOE_TOPK):
        rj = jnp.sum(jnp.where(lane == picks[j], prefix, 0.0), axis=1, keepdims=True)
        rk = jnp.where(olane == j, rj, rk)
    rk_ref[...] = rk.astype(jnp.int32)
    cnt_scr[...] = cnt_scr[...] + jnp.sum(onehot, axis=0, keepdims=True)
    cnt_ref[...] = cnt_scr[...]


def _route(h, w_router, b_router):
    m, d = h.shape
    n_exp = w_router.shape[1]
    tm = _pick(m, (256, 128, 64, 32, 16, 8))
    return pl.pallas_call(
        functools.partial(_router_kernel, n_exp=n_exp),
        grid=(m // tm,),
        in_specs=[pl.BlockSpec((tm, d), lambda i: (i, 0)),
                  pl.BlockSpec((d, n_exp), lambda i: (0, 0)),
                  pl.BlockSpec((1, n_exp), lambda i: (0, 0))],
        out_specs=[pl.BlockSpec((tm, LANE), lambda i: (i, 0))] * 3 + [pl.BlockSpec((1, n_exp), lambda i: (0, 0))],
        out_shape=[jax.ShapeDtypeStruct((m, LANE), jnp.int32), jax.ShapeDtypeStruct((m, LANE), F32),
                   jax.ShapeDtypeStruct((m, LANE), jnp.int32), jax.ShapeDtypeStruct((1, n_exp), F32)],
        scratch_shapes=[pltpu.VMEM((1, n_exp), F32)],
        compiler_params=_cparams(("arbitrary",), 32),
    )(h, w_router, b_router.reshape(1, n_exp))


def _dest_kernel(ti_ref, rk_ref, base_ref, o_ref, *, n_exp):
    tm = ti_ref.shape[0]
    lane = lax.broadcasted_iota(jnp.int32, (tm, n_exp), 1)
    olane = lax.broadcasted_iota(jnp.int32, (tm, LANE), 1)
    out = jnp.zeros((tm, LANE), jnp.int32)
    for j in range(MOE_TOPK):
        base = jnp.sum(jnp.where(lane == ti_ref[:, j:j + 1], base_ref[...], 0), axis=1, keepdims=True)
        out = jnp.where(olane == j, base + rk_ref[:, j:j + 1], out)
    o_ref[...] = out


def _dest_rows(ti, rk, base):
    m = ti.shape[0]
    n_exp = base.shape[1]
    tm = _pick(m, (256, 128, 64, 32, 16, 8))
    tile = pl.BlockSpec((tm, LANE), lambda i: (i, 0))
    return pl.pallas_call(
        functools.partial(_dest_kernel, n_exp=n_exp),
        grid=(m // tm,),
        in_specs=[tile, tile, pl.BlockSpec((1, n_exp), lambda i: (0, 0))],
        out_specs=tile,
        out_shape=jax.ShapeDtypeStruct((m, LANE), jnp.int32),
        compiler_params=_cparams(("arbitrary",), 32),
    )(ti, rk, base)


def _expert_kernel(be_ref, nu_ref, nv_ref, tok_hbm, h_hbm, wg_ref, wu_ref, wd_ref, y_ref,
                   x_buf, idx_smem, row_sem, idx_sem, *, bm):
    i, hf = pl.program_id(0), pl.program_id(1)
    n_used = nu_ref[0]

    def idx_copy(blk):
        return pltpu.make_async_copy(tok_hbm.at[blk], idx_smem, idx_sem)

    def row_copy(blk_slot, r, tok):
        return pltpu.make_async_copy(h_hbm.at[pl.ds(tok, 1)], x_buf.at[blk_slot, pl.ds(r, 1)],
                                     row_sem.at[blk_slot])

    def row_groups(blk):
        return (nv_ref[blk] + DMA_UNROLL - 1) // DMA_UNROLL

    def start_gather(blk):
        slot = blk % 2

        def body(g, _):
            for k in range(DMA_UNROLL):
                r = g * DMA_UNROLL + k
                row_copy(slot, r, idx_smem[r]).start()
            return 0
        lax.fori_loop(0, row_groups(blk), body, 0)

    def wait_gather(blk):
        slot = blk % 2

        def body(g, _):
            for k in range(DMA_UNROLL):
                row_copy(slot, g * DMA_UNROLL + k, 0).wait()
            return 0
        lax.fori_loop(0, row_groups(blk), body, 0)

    @pl.when(jnp.logical_and(i == 0, hf == 0))
    def _():
        x_buf[...] = jnp.zeros_like(x_buf)
        idx_copy(0).start()
        idx_copy(0).wait()
        start_gather(0)

    @pl.when(jnp.logical_and(i < n_used, hf == 0))
    def _():
        wait_gather(i)

    @pl.when(jnp.logical_and(i + 1 < n_used, hf == 0))
    def _():
        idx_copy(i + 1).start()

    @pl.when(jnp.logical_and(i + 1 < n_used, hf == 1))
    def _():
        idx_copy(i + 1).wait()
        start_gather(i + 1)

    def ffn_half():
        x = x_buf[i % 2].astype(BF16)
        g = jnp.dot(x, wg_ref[0].astype(BF16), preferred_element_type=F32)
        u = jnp.dot(x, wu_ref[0].astype(BF16), preferred_element_type=F32)
        a = (_silu(g) * u).astype(BF16)
        return jnp.dot(a, wd_ref[0].astype(BF16), preferred_element_type=F32)

    @pl.when(jnp.logical_and(i < n_used, hf == 0))
    def _():
        y_ref[...] = ffn_half()

    @pl.when(jnp.logical_and(i < n_used, hf == 1))
    def _():
        y_ref[...] = y_ref[...] + ffn_half()


def _experts(h, block_e, n_used, n_valid, row_tok, w_gate, w_up, w_down, bm):
    m, d = h.shape
    n_exp, _, de = w_gate.shape
    nb = block_e.shape[0]
    deh = de // 2
    last_used = lambda i, nu: jnp.minimum(i, nu[0] - 1)
    return pl.pallas_call(
        functools.partial(_expert_kernel, bm=bm),
        grid_spec=pltpu.PrefetchScalarGridSpec(
            num_scalar_prefetch=3,
            grid=(nb, 2),
            in_specs=[pl.BlockSpec(memory_space=pl.ANY),
                      pl.BlockSpec(memory_space=pl.ANY),
                      pl.BlockSpec((1, d, deh), lambda i, hf, be, nu, nv: (be[i], 0, hf)),
                      pl.BlockSpec((1, d, deh), lambda i, hf, be, nu, nv: (be[i], 0, hf)),
                      pl.BlockSpec((1, deh, d), lambda i, hf, be, nu, nv: (be[i], hf, 0))],
            out_specs=pl.BlockSpec((bm, d), lambda i, hf, be, nu, nv: (last_used(i, nu), 0)),
            scratch_shapes=[pltpu.VMEM((2, bm, d), F32), pltpu.SMEM((bm,), jnp.int32),
                            pltpu.SemaphoreType.DMA((2,)), pltpu.SemaphoreType.DMA(())]),
        out_shape=jax.ShapeDtypeStruct((nb * bm, d), F32),
        compiler_params=_cparams(("arbitrary", "arbitrary"), 56),
    )(block_e, n_used, n_valid, row_tok, h, w_gate, w_up, w_down)


def _dispatch(ti, rk, counts, bm):
    m = ti.shape[0]
    n_exp = counts.shape[1]
    n_assign = m * MOE_TOPK
    nb = -(-n_assign // bm) + n_exp
    nblk = (counts[0].astype(jnp.int32) + bm - 1) // bm
    blk_end = jnp.cumsum(nblk)
    blk_start = blk_end - nblk
    n_used = blk_end[-1]
    dest = _dest_rows(ti, rk, (blk_start * bm)[None, :])[:, :MOE_TOPK]
    tok = jnp.arange(n_assign, dtype=jnp.int32) // MOE_TOPK
    row_tok = jnp.zeros((nb * bm,), jnp.int32).at[dest.reshape(-1)].set(tok, unique_indices=True)
    blk = jnp.arange(nb, dtype=jnp.int32)
    be = jnp.minimum(jnp.searchsorted(blk_end, blk, side='right'), n_exp - 1).astype(jnp.int32)
    n_valid = jnp.clip(counts[0].astype(jnp.int32)[be] - (blk - blk_start[be]) * bm, 0, bm)
    n_valid = jnp.where(blk < n_used, n_valid, 0).astype(jnp.int32)
    be = jnp.where(blk < n_used, be, be[jnp.maximum(n_used - 1, 0)])
    return be, n_used.reshape(1).astype(jnp.int32), n_valid, row_tok.reshape(nb, bm), dest


def _glu_kernel(h_ref, wg_ref, wu_ref, o_ref):
    x = h_ref[...].astype(BF16)
    g = jnp.dot(x, wg_ref[...].astype(BF16), preferred_element_type=F32)
    u = jnp.dot(x, wu_ref[...].astype(BF16), preferred_element_type=F32)
    o_ref[...] = (_silu(g) * u).astype(o_ref.dtype)


def _glu(h, wg, wu):
    m, d = h.shape
    de = wg.shape[1]
    bm = _pick(m, (528, 512, 256, 128, 64, 32, 16, 8))
    bn = _pick(de, (256, 128))
    return pl.pallas_call(
        _glu_kernel,
        grid=(m // bm, de // bn),
        in_specs=[pl.BlockSpec((bm, d), lambda i, j: (i, 0)),
                  pl.BlockSpec((d, bn), lambda i, j: (0, j)),
                  pl.BlockSpec((d, bn), lambda i, j: (0, j))],
        out_specs=pl.BlockSpec((bm, bn), lambda i, j: (i, j)),
        out_shape=jax.ShapeDtypeStruct((m, de), BF16),
        compiler_params=_cparams(("arbitrary", "arbitrary"), 48),
    )(h, wg, wu)


def _combine_kernel(pos_hbm, y_hbm, tw_ref, sh_ref, x_ref, gate_ref, gam_ref, bet_ref, o_ref,
                    buf, idx_smem, row_sem, idx_sem, *, tt):
    b, j = pl.program_id(0), pl.program_id(1)
    step = b * pl.num_programs(1) + j
    n = tt * MOE_TOPK
    cp = pltpu.make_async_copy(pos_hbm.at[step], idx_smem, idx_sem)
    cp.start()
    cp.wait()

    def row_copy(a, src):
        return pltpu.make_async_copy(y_hbm.at[pl.ds(src, 1)], buf.at[a % MOE_TOPK, pl.ds(a // MOE_TOPK, 1)],
                                     row_sem)

    def start(a, _):
        row_copy(a, idx_smem[a]).start()
        return 0
    lax.fori_loop(0, n, start, 0, unroll=DMA_UNROLL)

    for e in range(MOE_TOPK):
        pltpu.make_async_copy(y_hbm.at[pl.ds(0, tt)], buf.at[e], row_sem).wait()

    tw = tw_ref[0]
    routed = buf[0] * tw[:, 0:1]
    for e in range(1, MOE_TOPK):
        routed = routed + buf[e] * tw[:, e:e + 1]
    z = DN_ALPHA * x_ref[0] + gate_ref[0] * (routed + sh_ref[0])
    o_ref[0] = _layer_norm(z) * gam_ref[...] + bet_ref[...]


def _combine(pos, y_sorted, top_w, shared, x1, gate, gamma, beta):
    b, t, d = x1.shape
    tt = _pick(t, (64, 32, 16, 8))
    n = tt * MOE_TOPK
    pos_steps = pos.reshape(-1, n)
    row = lambda w: pl.BlockSpec((1, tt, w), lambda i, j: (i, j, 0))
    mod = pl.BlockSpec((1, 1, d), lambda i, j: (i, 0, 0))
    vec = pl.BlockSpec((1, d), lambda i, j: (0, 0))
    return pl.pallas_call(
        functools.partial(_combine_kernel, tt=tt),
        grid=(b, t // tt),
        in_specs=[pl.BlockSpec(memory_space=pl.ANY), pl.BlockSpec(memory_space=pl.ANY),
                  row(MOE_TOPK), row(d), row(d), mod, vec, vec],
        out_specs=row(d),
        out_shape=jax.ShapeDtypeStruct((b, t, d), F32),
        scratch_shapes=[pltpu.VMEM((MOE_TOPK, tt, d), F32), pltpu.SMEM((n,), jnp.int32),
                        pltpu.SemaphoreType.DMA(()), pltpu.SemaphoreType.DMA(())],
        compiler_params=_cparams(("arbitrary", "arbitrary"), 40),
    )(pos_steps, y_sorted, top_w, shared, x1, gate, gamma.reshape(1, d), beta.reshape(1, d))


def kernel(x_prompt, x_sample, c_prompt, c_sample, cache_k, cache_v, cache_idx_k, state_ret, page_table, w_cond, b_cond, w_in, w_ret_o, w_att_o, w_out, ln_mix_g, ln_mix_b, w_router, b_router, w_exp_gate, w_exp_up, w_exp_down, w_sh_gate, w_sh_up, w_sh_down, ln_ffn_g, ln_ffn_b):
    bp, tp, d = x_prompt.shape
    bs, ts, _ = x_sample.shape
    assert bp == 1, "prompt group kernels are written for a single prompt sequence"
    n_pool, page, ah, dh = cache_k.shape
    idh = cache_idx_k.shape[2]
    rh, dk, dv = state_ret.shape[1:]
    n_pages = page_table.shape[1]
    past = n_pages * page
    n_exp = w_router.shape[1]
    aw, rqk, rv_w = ah * dh, rh * dk, rh * dv
    d_in = w_in.shape[1]
    ih = d_in - (2 * rqk + 2 * rv_w + 3 * aw + idh + 2 * d)
    ih = ih // (idh + 1)
    iq_w = ih * idh
    widths = (rqk, rqk, rv_w, rv_w, aw, aw, aw, iq_w, idh, ih, d, d)
    assert sum(widths) == d_in
    offs = np.concatenate([[0], np.cumsum(widths)]).tolist()
    o_rq, o_rk, o_rv, o_rg, o_aq, o_ak, o_av, o_iq, o_ik, o_iw, o_ga, o_gb = offs[:12]
    mp, ms = bp * tp, bs * ts
    m = mp + ms

    mod = _modulation(jnp.concatenate([c_prompt, c_sample], axis=0), w_cond, b_cond)
    mods_p = [v[:, None, :] for v in jnp.split(mod[:bp], 6, axis=-1)]
    mods_s = [v[:, None, :] for v in jnp.split(mod[bp:], 6, axis=-1)]

    h = jnp.concatenate([_ln_mod(x_prompt, mods_p[1], mods_p[0], BF16).reshape(mp, d),
                         _ln_mod(x_sample, mods_s[1], mods_s[0], BF16).reshape(ms, d)], axis=0)

    pos = jnp.concatenate([jnp.tile(jnp.arange(tp, dtype=jnp.int32), bp),
                           jnp.tile(past + jnp.arange(ts, dtype=jnp.int32), bs)]).astype(F32)
    half = dk // 2
    inv = ROT_BASE ** (-jnp.arange(half, dtype=F32) / half)
    ang = pos[:, None] * inv[None, :]
    cos, sin = jnp.cos(ang), jnp.sin(ang)

    rq, = _proj(h, w_in, o_rq, rqk, [BF16], _ep_rotate(dk, None), extras=(cos, sin))
    rk, = _proj(h, w_in, o_rk, rqk, [BF16], _ep_rotate(dk, dk ** -0.5), extras=(cos, sin))
    rv, = _proj(h, w_in, o_rv, rv_w, [BF16], _ep_store())
    rg, = _proj(h, w_in, o_rg, rv_w, [F32], _ep_store())
    aq, = _proj(h, w_in, o_aq, aw, [BF16], _ep_store())
    ak, ak_b = _proj(h, w_in, o_ak, aw, [F32, BF16], _ep_store())
    av, av_b = _proj(h, w_in, o_av, aw, [F32, BF16], _ep_store())
    iq, = _proj(h, w_in, o_iq, iq_w, [BF16], _ep_store(idh ** -0.5))
    ik, ik_b = _proj(h, w_in, o_ik, idh, [F32, BF16], _ep_store(), bn=LANE)
    iw, = _proj(h, w_in, o_iw, LANE, [F32], _ep_store(ih ** -0.5), bn=LANE)
    w_gates = w_in[:, o_ga:]
    ga, = _proj(h, w_gates, 0, d, [F32], _ep_store())
    gb, = _proj(h, w_gates, d, d, [F32], _ep_store())

    chunk = _pick(tp, (256, 128))
    ra_p, s_p = _retention(rq, rk, rv, rg, jnp.zeros((bp, rh, dk, dv), F32), bp, tp, chunk, chunk, rh, dk, dv)
    pad_s = lambda a: jnp.pad(a[mp:].reshape(bs, ts, -1), ((0, 0), (0, LANE - ts), (0, 0))).reshape(bs * LANE, -1)
    ra_s, s_s = _retention(pad_s(rq), pad_s(rk), pad_s(rv), pad_s(rg), state_ret.astype(F32),
                           bs, LANE, LANE, ts, rh, dk, dv)
    ra_s = ra_s.reshape(bs, LANE, rv_w)[:, :ts]

    tq = _pick(tp, (256, 128))
    tk = _pick(tp, (512, 256, 128))
    bias_p = _select_prompt(iq, ik_b[:mp], iw, tp, ih, idh, min(TOPK_MAX, tp // 4), tq, tk)
    att_p = _attention_prompt(aq, ak_b, av_b, bias_p, tp, ah, dh, tq, tk)

    pp = _pick(n_pages, (4, 2, 1))
    hm = lambda a, nh: a.reshape(bs, ts, nh, -1).transpose(0, 2, 1, 3).reshape(bs, nh * ts, -1)
    iq_rows = hm(iq[mp:], ih)
    iw_rows = iw[mp:, :ih].reshape(bs, ts, ih).transpose(0, 2, 1).reshape(bs, ih * ts, 1)
    pad_new = lambda a: jnp.pad(a[mp:].reshape(bs, ts, -1), ((0, 0), (0, LANE - ts), (0, 0)))
    bias_past, bias_new = _select_sample(page_table, cache_idx_k, iq_rows, iw_rows, pad_new(ik_b),
                                         ts, ih, min(TOPK_MAX, (past + ts) // 4), _pick(n_pages, (16, 8, 4, 2, 1)))
    flat_kv = lambda c: c.reshape(n_pool * page * ah, dh)
    new_kv = lambda a: jnp.pad(a[mp:].reshape(bs, ts * ah, dh), ((0, 0), (0, (page - ts) * ah), (0, 0)))
    att_s = _attention_sample(page_table, flat_kv(cache_k), flat_kv(cache_v), hm(aq[mp:], ah).astype(F32),
                              bias_past, bias_new, new_kv(ak), new_kv(av), ts, ah, dh, pp)
    att_s = att_s.reshape(bs, ah, ts, dh).transpose(0, 2, 1, 3)

    ra = jnp.concatenate([ra_p, ra_s.reshape(ms, rv_w)], axis=0)
    att = jnp.concatenate([att_p, att_s.reshape(ms, aw)], axis=0)
    u = _merge(ra, w_ret_o, att, w_att_o, ga, gb)
    mix, = _proj(u, w_out, 0, d, [F32], _ep_store())
    x1_p, h2_p = _post_ln(x_prompt, mix.reshape(1, m, d), mods_p[2], ln_mix_g, ln_mix_b, mods_p[4], mods_p[3])
    x1_s, h2_s = _post_ln(x_sample, mix[mp:].reshape(bs, ts, d), mods_s[2], ln_mix_g, ln_mix_b, mods_s[4], mods_s[3])
    h2 = jnp.concatenate([h2_p.reshape(mp, d), h2_s.reshape(ms, d)], axis=0)

    top_i, top_w, rank, counts = _route(h2, w_router, b_router)
    top_w = top_w[:, :MOE_TOPK]
    n_assign = m * MOE_TOPK
    bm = 320 if n_assign // n_exp >= 256 else _pick(max(n_assign // n_exp, 8), (128, 64, 32, 16, 8))
    block_e, n_used, n_valid, row_tok, dest = _dispatch(top_i, rank, counts, bm)
    y_sorted = _experts(h2, block_e, n_used, n_valid, row_tok, w_exp_gate, w_exp_up, w_exp_down, bm)
    shared, = _proj(_glu(h2, w_sh_gate, w_sh_up), w_sh_down, 0, d, [F32], _ep_store())

    y_p = _combine(dest[:mp], y_sorted, top_w[:mp].reshape(bp, tp, MOE_TOPK), shared.reshape(1, m, d),
                   x1_p, mods_p[5], ln_ffn_g, ln_ffn_b)
    y_s = _combine(dest[mp:], y_sorted, top_w[mp:].reshape(bs, ts, MOE_TOPK), shared[mp:].reshape(bs, ts, d),
                   x1_s, mods_s[5], ln_ffn_g, ln_ffn_b)

    return (y_p, y_s,
            ak[:mp].reshape(bp, tp, ah, dh), av[:mp].reshape(bp, tp, ah, dh), ik[:mp].reshape(bp, tp, idh),
            s_p.reshape(bp, rh, dk, dv),
            ak[mp:].reshape(bs, ts, ah, dh), av[mp:].reshape(bs, ts, ah, dh), ik[mp:].reshape(bs, ts, idh),
            s_s)
```

```python
import functools
import math

import jax
import jax.numpy as jnp
import numpy as np
from jax import lax
from jax.experimental import pallas as pl
from jax.experimental.pallas import tpu as pltpu

ROT_BASE = 10000.0
TOPK_MAX = 256
Q_BLOCK = 128
MOE_TOPK = 8
N_GROUPS = 8
TOPK_GROUPS = 4
ROUTED_SCALE = 2.5
LN_EPS = 1e-5
DEPTH = 1
DN_ALPHA = (2.0 * DEPTH) ** 0.25

LANE = 128
SUBLANE = 8
VMEM_BYTES_V7X = 64 * 2 ** 20

DMA_UNROLL = 8

NEG = -1e30
INT_MIN = -2 ** 31
BF16 = jnp.bfloat16
F32 = jnp.float32

_NT = (((1,), (1,)), ((), ()))
_TN = (((0,), (0,)), ((), ()))


def _cparams(sem, vmem_mb):
    assert vmem_mb * 2 ** 20 < VMEM_BYTES_V7X
    return pltpu.CompilerParams(dimension_semantics=sem, vmem_limit_bytes=vmem_mb * 2 ** 20)


def _pick(n, cands):
    for c in cands:
        if c <= n and n % c == 0:
            return c
    return n


def _silu(x):
    return x * jax.nn.sigmoid(x)


def _layer_norm(x):
    mu = jnp.mean(x, axis=-1, keepdims=True)
    xc = x - mu
    var = jnp.mean(xc * xc, axis=-1, keepdims=True)
    return xc * lax.rsqrt(var + LN_EPS)


def _cond_kernel(c_ref, w_ref, b_ref, o_ref):
    a = _silu(c_ref[...]).astype(BF16)
    o_ref[...] = jnp.dot(a, w_ref[...].astype(BF16), preferred_element_type=F32) + b_ref[...]


def _modulation(c, w_cond, b_cond):
    nb, d = c.shape
    n = w_cond.shape[1]
    mp = -(-nb // SUBLANE) * SUBLANE
    cp = jnp.pad(c, ((0, mp - nb), (0, 0)))
    tn = _pick(n, (512, 256, 128))
    out = pl.pallas_call(
        _cond_kernel,
        grid=(n // tn,),
        in_specs=[pl.BlockSpec((mp, d), lambda j: (0, 0)),
                  pl.BlockSpec((d, tn), lambda j: (0, j)),
                  pl.BlockSpec((1, tn), lambda j: (0, j))],
        out_specs=pl.BlockSpec((mp, tn), lambda j: (0, j)),
        out_shape=jax.ShapeDtypeStruct((mp, n), F32),
        compiler_params=_cparams(("arbitrary",), 40),
    )(cp, w_cond, b_cond.reshape(1, n))
    return out[:nb]


def _ln_mod_kernel(x_ref, sc_ref, sh_ref, o_ref):
    y = _layer_norm(x_ref[0])
    o_ref[0] = (y * (1.0 + sc_ref[0]) + sh_ref[0]).astype(o_ref.dtype)


def _ln_mod(x, scale, shift, out_dtype):
    b, t, d = x.shape
    tt = _pick(t, (256, 128, 64, 32, 16, 8))
    mod = pl.BlockSpec((1, 1, d), lambda i, j: (i, 0, 0))
    row = pl.BlockSpec((1, tt, d), lambda i, j: (i, j, 0))
    return pl.pallas_call(
        _ln_mod_kernel,
        grid=(b, t // tt),
        in_specs=[row, mod, mod],
        out_specs=row,
        out_shape=jax.ShapeDtypeStruct((b, t, d), out_dtype),
        compiler_params=_cparams(("arbitrary", "arbitrary"), 40),
    )(x, scale, shift)


def _post_ln_kernel(x_ref, y_ref, gate_ref, gam_ref, bet_ref, sc_ref, sh_ref, x1_ref, h_ref):
    z = DN_ALPHA * x_ref[0] + gate_ref[0] * y_ref[0]
    x1 = _layer_norm(z) * gam_ref[...] + bet_ref[...]
    x1_ref[0] = x1
    h_ref[0] = (_layer_norm(x1) * (1.0 + sc_ref[0]) + sh_ref[0]).astype(h_ref.dtype)


def _post_ln(x, y, gate, gamma, beta, scale, shift):
    b, t, d = x.shape
    tt = _pick(t, (256, 128, 64, 32, 16, 8))
    mod = pl.BlockSpec((1, 1, d), lambda i, j: (i, 0, 0))
    row = pl.BlockSpec((1, tt, d), lambda i, j: (i, j, 0))
    vec = pl.BlockSpec((1, d), lambda i, j: (0, 0))
    return pl.pallas_call(
        _post_ln_kernel,
        grid=(b, t // tt),
        in_specs=[row, row, mod, vec, vec, mod, mod],
        out_specs=[row, row],
        out_shape=[jax.ShapeDtypeStruct((b, t, d), F32), jax.ShapeDtypeStruct((b, t, d), F32)],
        compiler_params=_cparams(("arbitrary", "arbitrary"), 48),
    )(x, y, gate, gamma.reshape(1, d), beta.reshape(1, d), scale, shift)


def _proj_kernel(*refs, n_extra, n_out, epilogue):
    a_ref, w_ref = refs[0], refs[1]
    extra = refs[2:2 + n_extra]
    outs = refs[2 + n_extra:2 + n_extra + n_out]
    acc = jnp.dot(a_ref[...], w_ref[...].astype(BF16), preferred_element_type=F32)
    epilogue(acc, extra, outs)


def _proj(a, w, col0, ncols, out_dtypes, epilogue, extras=(), bm=None, bn=None, vmem_mb=52):
    m, k = a.shape
    bm = bm or _pick(m, (1056, 1024, 768, 512, 256, 128, 64, 32, 16, 8))
    bn = bn or _pick(math.gcd(ncols, col0) if col0 else ncols, (512, 256, 128))
    assert col0 % bn == 0 and ncols % bn == 0 and m % bm == 0
    cb = col0 // bn
    in_specs = [pl.BlockSpec((bm, k), lambda i, j: (i, 0)),
                pl.BlockSpec((k, bn), lambda i, j: (0, cb + j))]
    for e in extras:
        in_specs.append(pl.BlockSpec((bm, e.shape[1]), lambda i, j: (i, 0)))
    out_specs = [pl.BlockSpec((bm, bn), lambda i, j: (i, j)) for _ in out_dtypes]
    out_shape = [jax.ShapeDtypeStruct((m, ncols), dt) for dt in out_dtypes]
    kern = functools.partial(_proj_kernel, n_extra=len(extras), n_out=len(out_dtypes), epilogue=epilogue)
    return pl.pallas_call(
        kern,
        grid=(m // bm, ncols // bn),
        in_specs=in_specs,
        out_specs=out_specs,
        out_shape=out_shape,
        compiler_params=_cparams(("arbitrary", "arbitrary"), vmem_mb),
    )(a, w, *extras)


def _ep_store(scale=None):
    def ep(acc, extra, outs):
        v = acc if scale is None else acc * scale
        for o in outs:
            o[...] = v.astype(o.dtype)
    return ep


def _ep_rotate(dk, scale):
    half = dk // 2

    def ep(acc, extra, outs):
        cos, sin = extra[0][...], extra[1][...]
        (o,) = outs
        for h in range(acc.shape[1] // dk):
            x1 = acc[:, h * dk:h * dk + half]
            x2 = acc[:, h * dk + half:(h + 1) * dk]
            r1 = x1 * cos - x2 * sin
            r2 = x2 * cos + x1 * sin
            if scale is not None:
                r1, r2 = r1 * scale, r2 * scale
            o[:, h * dk:h * dk + half] = r1.astype(o.dtype)
            o[:, h * dk + half:(h + 1) * dk] = r2.astype(o.dtype)
    return ep


def _ret_kernel(q_ref, k_ref, v_ref, g_ref, s0_ref, intra_ref, qd_ref, kd_ref, cd_ref,
                o_ref, s_out_ref, s_scr):
    c = pl.program_id(2)

    @pl.when(c == 0)
    def _():
        s_scr[...] = s0_ref[0, 0]

    q, k, v = q_ref[...], k_ref[...], v_ref[...]
    s = s_scr[...]
    att = lax.dot_general(q, k, _NT, preferred_element_type=F32) * intra_ref[0]
    o = jnp.dot(att.astype(BF16), v, preferred_element_type=F32)
    o = o + jnp.dot((q.astype(F32) * qd_ref[0]).astype(BF16), s.astype(BF16), preferred_element_type=F32)
    kd = (k.astype(F32) * kd_ref[0]).astype(BF16)
    s_new = s * cd_ref[0] + lax.dot_general(kd, v, _TN, preferred_element_type=F32)
    s_scr[...] = s_new
    o_ref[...] = (_silu(g_ref[...]) * _layer_norm(o)).astype(o_ref.dtype)

    @pl.when(c == pl.num_programs(2) - 1)
    def _():
        s_out_ref[0, 0] = s_new


def _ret_tables(nh, chunk, n_valid, dv):
    lg = jnp.log1p(-jnp.exp2(-5.0 - jnp.arange(nh, dtype=F32)))
    idx = jnp.arange(chunk, dtype=F32)
    diff = idx[:, None] - idx[None, :]
    intra = jnp.where(diff >= 0, jnp.exp(lg[:, None, None] * jnp.maximum(diff, 0.0)), 0.0)
    q_dec = jnp.exp(lg[:, None] * (idx[None, :] + 1.0))[:, :, None]
    k_dec = jnp.where(idx[None, :] < n_valid, jnp.exp(lg[:, None] * jnp.maximum(n_valid - 1.0 - idx[None, :], 0.0)),
                      0.0)[:, :, None]
    c_dec = jnp.broadcast_to(jnp.exp(lg * n_valid)[:, None, None], (nh, 1, dv))
    return intra, q_dec, k_dec, c_dec


def _retention(q, k, v, g, s0, nb, t, chunk, n_valid, nh, dk, dv):
    nc = t // chunk
    intra, q_dec, k_dec, c_dec = _ret_tables(nh, chunk, n_valid, dv)
    tab = lambda shp: pl.BlockSpec((1,) + shp, lambda b, h, c: (h, 0, 0))
    row = lambda w: pl.BlockSpec((chunk, w), lambda b, h, c: (b * nc + c, h))
    st = pl.BlockSpec((1, 1, dk, dv), lambda b, h, c: (b, h, 0, 0))
    return pl.pallas_call(
        _ret_kernel,
        grid=(nb, nh, nc),
        in_specs=[row(dk), row(dk), row(dv), row(dv), st,
                  tab((chunk, chunk)), tab((chunk, 1)), tab((chunk, 1)), tab((1, dv))],
        out_specs=[row(dv), st],
        out_shape=[jax.ShapeDtypeStruct((nb * t, nh * dv), BF16),
                   jax.ShapeDtypeStruct((nb, nh, dk, dv), F32)],
        scratch_shapes=[pltpu.VMEM((dk, dv), F32)],
        compiler_params=_cparams(("arbitrary", "arbitrary", "arbitrary"), 32),
    )(q, k, v, g, s0, intra, q_dec, k_dec, c_dec)


def _to_key(score):
    bits = pltpu.bitcast(score + 0.0, jnp.int32)
    return jnp.where(bits >= 0, bits, bits ^ jnp.int32(0x7FFFFFFF))


def _fold_lanes(x):
    acc = x[:, :LANE]
    for g in range(1, x.shape[1] // LANE):
        acc = acc + x[:, g * LANE:(g + 1) * LANE]
    return acc


def _kth_largest_key(key_scr, n_tiles, rows, n_sel):
    def count_ge(cand):
        def body(kt, cnt):
            return cnt + _fold_lanes((key_scr[kt] >= cand).astype(jnp.int32))
        cnt = lax.fori_loop(0, n_tiles, body, jnp.zeros((rows, LANE), jnp.int32))
        return jnp.sum(cnt, axis=1, keepdims=True)

    def bit_step(b, t):
        cand = t + jnp.left_shift(jnp.int32(1), 31 - b)
        return jnp.where(count_ge(cand) >= n_sel, cand, t)

    return lax.fori_loop(0, 32, bit_step, jnp.full((rows, 1), INT_MIN, jnp.int32))


def _tie_cutoff(key_scr, n_tiles, rows, w, n_sel, thr):
    def counts(kt, c):
        k = key_scr[kt]
        gt = c[0] + _fold_lanes((k > thr).astype(jnp.int32))
        eq = c[1] + _fold_lanes((k == thr).astype(jnp.int32))
        return gt, eq
    z = jnp.zeros((rows, LANE), jnp.int32)
    gt, eq = lax.fori_loop(0, n_tiles, counts, (z, z))
    need = n_sel - jnp.sum(gt, axis=1, keepdims=True)
    n_eq = jnp.sum(eq, axis=1, keepdims=True)

    def search(_):
        def count_eq_le(cut):
            def body(kt, cnt):
                col = kt * w + lax.broadcasted_iota(jnp.int32, (rows, w), 1)
                hit = jnp.where(col <= cut, (key_scr[kt] == thr).astype(jnp.int32), 0)
                return cnt + _fold_lanes(hit)
            cnt = lax.fori_loop(0, n_tiles, body, jnp.zeros((rows, LANE), jnp.int32))
            return jnp.sum(cnt, axis=1, keepdims=True)

        def bit_step(b, cut):
            cand = cut - jnp.left_shift(jnp.int32(1), 30 - b)
            ok = jnp.logical_and(cand >= 0, count_eq_le(cand) >= need)
            return jnp.where(ok, cand, cut)
        return lax.fori_loop(0, 31, bit_step, jnp.full((rows, 1), 2 ** 31 - 1, jnp.int32))

    excess = jnp.max(jnp.where(n_eq > need, 1, 0)) > 0
    return lax.cond(excess, search, lambda _: jnp.full((rows, 1), 2 ** 31 - 1, jnp.int32), 0)


def _select_bias(key, thr, cut, col, valid):
    bias = jnp.where(key > thr, 0.0, jnp.where(key == thr, jnp.where(col <= cut, 0.0, NEG), NEG))
    return bias if valid is None else jnp.where(valid, bias, NEG)


def _sel_prompt_kernel(iq_ref, ik_ref, iw_ref, bias_ref, key_scr, *, tq, tk, ih, idh, n_sel):
    qi = pl.program_id(0)
    nk_total = bias_ref.shape[1]
    n_tiles = ((qi + 1) * tq + tk - 1) // tk
    row = qi * tq + lax.broadcasted_iota(jnp.int32, (tq, tk), 0)
    lane = lax.broadcasted_iota(jnp.int32, (tq, tk), 1)

    def score_tile(kt, _):
        ik = ik_ref[pl.ds(pl.multiple_of(kt * tk, tk), tk), :]
        acc = jnp.zeros((tq, tk), F32)
        for h in range(ih):
            r = lax.dot_general(iq_ref[:, h * idh:(h + 1) * idh], ik, _NT, preferred_element_type=F32)
            acc = acc + jnp.maximum(r, 0.0) * iw_ref[:, h:h + 1]
        key_scr[kt] = jnp.where(kt * tk + lane <= row, _to_key(acc), INT_MIN)
        return 0
    lax.fori_loop(0, n_tiles, score_tile, 0)

    thr = _kth_largest_key(key_scr, n_tiles, tq, n_sel)
    cut = _tie_cutoff(key_scr, n_tiles, tq, tk, n_sel, thr)

    def emit(kt, _):
        col = kt * tk + lane
        bias_ref[0, kt] = _select_bias(key_scr[kt], thr, cut, col, col <= row)
        return 0
    lax.fori_loop(0, n_tiles, emit, 0)

    def fill(kt, _):
        bias_ref[0, kt] = jnp.full((tq, tk), NEG, F32)
        return 0
    lax.fori_loop(n_tiles, nk_total, fill, 0)


def _select_prompt(iq, ik, iw, t, ih, idh, n_sel, tq, tk):
    nq, nk = t // tq, t // tk
    kern = functools.partial(_sel_prompt_kernel, tq=tq, tk=tk, ih=ih, idh=idh, n_sel=n_sel)
    return pl.pallas_call(
        kern,
        grid=(nq,),
        in_specs=[pl.BlockSpec((tq, ih * idh), lambda i: (i, 0)),
                  pl.BlockSpec((t, idh), lambda i: (0, 0)),
                  pl.BlockSpec((tq, iw.shape[1]), lambda i: (i, 0))],
        out_specs=pl.BlockSpec((1, nk, tq, tk), lambda i: (i, 0, 0, 0)),
        out_shape=jax.ShapeDtypeStruct((nq, nk, tq, tk), F32),
        scratch_shapes=[pltpu.VMEM((nk, tq, tk), jnp.int32)],
        compiler_params=_cparams(("arbitrary",), 48),
    )(iq, ik, iw)


def _attn_prompt_kernel(q_ref, k_ref, v_ref, b_ref, o_ref, m_scr, l_scr, acc_scr, *, tq, tk, nh, dh, scale):
    i, j = pl.program_id(0), pl.program_id(1)
    last = ((i + 1) * tq - 1) // tk

    @pl.when(j == 0)
    def _():
        m_scr[...] = jnp.full(m_scr.shape, -jnp.inf, F32)
        l_scr[...] = jnp.zeros_like(l_scr)
        acc_scr[...] = jnp.zeros_like(acc_scr)

    @pl.when(j <= last)
    def _():
        nsub = b_ref.shape[0]
        bias = b_ref[0, 0] if nsub == 1 else jnp.concatenate([b_ref[a, 0] for a in range(nsub)], axis=0)
        for h in range(nh):
            sl = slice(h * dh, (h + 1) * dh)
            s = lax.dot_general(q_ref[:, sl], k_ref[:, sl], _NT, preferred_element_type=F32) * scale + bias
            m_prev = m_scr[h]
            m_new = jnp.maximum(m_prev, jnp.max(s, axis=1, keepdims=True))
            alpha = jnp.exp(m_prev - m_new)
            p = jnp.exp(s - m_new)
            l_scr[h] = alpha * l_scr[h] + jnp.sum(p, axis=1, keepdims=True)
            acc_scr[:, sl] = alpha * acc_scr[:, sl] + jnp.dot(p.astype(BF16), v_ref[:, sl],
                                                             preferred_element_type=F32)
            m_scr[h] = m_new

    @pl.when(j == last)
    def _():
        for h in range(nh):
            sl = slice(h * dh, (h + 1) * dh)
            o_ref[:, sl] = (acc_scr[:, sl] / l_scr[h]).astype(o_ref.dtype)


def _attention_prompt(q, k, v, bias, t, nh, dh, tq, tk):
    nq, nk = t // tq, t // tk
    nsub = tq // bias.shape[2]
    kdiv = bias.shape[3] // tk
    last = lambda i: ((i + 1) * tq - 1) // tk
    kcl = lambda i, j: jnp.minimum(j, last(i))
    kern = functools.partial(_attn_prompt_kernel, tq=tq, tk=tk, nh=nh, dh=dh, scale=dh ** -0.5)
    return pl.pallas_call(
        kern,
        grid=(nq, nk),
        in_specs=[pl.BlockSpec((tq, nh * dh), lambda i, j: (i, 0)),
                  pl.BlockSpec((tk, nh * dh), lambda i, j: (jnp.minimum(j, last(i)), 0)),
                  pl.BlockSpec((tk, nh * dh), lambda i, j: (jnp.minimum(j, last(i)), 0)),
                  pl.BlockSpec((nsub, 1, tq // nsub, tk),
                               lambda i, j: (i, kcl(i, j) // kdiv, 0, kcl(i, j) % kdiv))],
        out_specs=pl.BlockSpec((tq, nh * dh), lambda i, j: (i, 0)),
        out_shape=jax.ShapeDtypeStruct((t, nh * dh), BF16),
        scratch_shapes=[pltpu.VMEM((nh, tq, 1), F32), pltpu.VMEM((nh, tq, 1), F32),
                        pltpu.VMEM((tq, nh * dh), F32)],
        compiler_params=_cparams(("arbitrary", "arbitrary"), 48),
    )(q, k, v, bias)


def _sel_sample_kernel(pt_ref, *refs, pp, t, ih, n_sel):
    pages = refs[:pp]
    iq_ref, iw_ref, ikn_ref, bp_ref, bn_ref, key_scr = refs[pp:]
    s = pl.program_id(1)
    n_past = bp_ref.shape[1]
    iq = iq_ref[0]
    iw = iw_ref[0]

    def scores(ik):
        r = lax.dot_general(iq, ik, _NT, preferred_element_type=F32)
        r = jnp.maximum(r, 0.0) * iw
        acc = r[0:t]
        for h in range(1, ih):
            acc = acc + r[h * t:(h + 1) * t]
        return acc

    for p in range(pp):
        key_scr[s * pp + p] = _to_key(scores(pages[p][0].astype(BF16)))

    @pl.when(s == pl.num_programs(1) - 1)
    def _():
        trow = lax.broadcasted_iota(jnp.int32, (t, LANE), 0)
        lane = lax.broadcasted_iota(jnp.int32, (t, LANE), 1)
        new_ok = lane <= trow
        key_scr[n_past] = jnp.where(new_ok, _to_key(scores(ikn_ref[0])), INT_MIN)
        thr = _kth_largest_key(key_scr, n_past + 1, t, n_sel)
        cut = _tie_cutoff(key_scr, n_past + 1, t, LANE, n_sel, thr)

        def emit(kt, _):
            bp_ref[0, kt] = _select_bias(key_scr[kt], thr, cut, kt * LANE + lane, None)
            return 0
        lax.fori_loop(0, n_past, emit, 0)
        bn_ref[0] = _select_bias(key_scr[n_past], thr, cut, n_past * LANE + lane, new_ok)


def _select_sample(page_table, cache_idx_k, iq_rows, iw_rows, ik_new, t, ih, n_sel, pp):
    b, n_pages = page_table.shape
    _, page, idh = cache_idx_k.shape
    assert page == LANE and t <= LANE
    r = ih * t
    page_spec = lambda p: pl.BlockSpec((1, page, idh), lambda i, s, pt: (pt[i, s * pp + p], 0, 0))
    per_b = lambda shp: pl.BlockSpec((1,) + shp, lambda i, s, pt: (i,) + (0,) * len(shp))
    kern = functools.partial(_sel_sample_kernel, pp=pp, t=t, ih=ih, n_sel=n_sel)
    return pl.pallas_call(
        kern,
        grid_spec=pltpu.PrefetchScalarGridSpec(
            num_scalar_prefetch=1,
            grid=(b, n_pages // pp),
            in_specs=[page_spec(p) for p in range(pp)] + [per_b((r, idh)), per_b((r, 1)), per_b((LANE, idh))],
            out_specs=[per_b((n_pages, t, LANE)), per_b((t, LANE))],
            scratch_shapes=[pltpu.VMEM((n_pages + 1, t, LANE), jnp.int32)]),
        out_shape=[jax.ShapeDtypeStruct((b, n_pages, t, LANE), F32),
                   jax.ShapeDtypeStruct((b, t, LANE), F32)],
        compiler_params=_cparams(("arbitrary", "arbitrary"), 32),
    )(page_table, *([cache_idx_k] * pp), iq_rows, iw_rows, ik_new)


def _attn_sample_kernel(pt_ref, *refs, pp, t, nh, dh, scale):
    kp, vp = refs[:pp], refs[pp:2 * pp]
    q_ref, bp_ref, bn_ref, kn_ref, vn_ref, o_ref, m_scr, l_scr, acc_scr = refs[2 * pp:]
    s = pl.program_id(1)
    page = kp[0].shape[0] // nh

    @pl.when(s == 0)
    def _():
        m_scr[...] = jnp.full(m_scr.shape, -jnp.inf, F32)
        l_scr[...] = jnp.zeros_like(l_scr)
        acc_scr[...] = jnp.zeros_like(acc_scr)

    def head_rows(ref, h):
        return ref[pl.ds(h, page, stride=nh), :].astype(BF16)

    def update(k_refs, v_refs, bias):
        cols = []
        for kr in k_refs:
            parts = [lax.dot_general(q_ref[0, h * t:(h + 1) * t, :].astype(BF16), head_rows(kr, h), _NT,
                                     preferred_element_type=F32) for h in range(nh)]
            cols.append(jnp.concatenate(parts, axis=0))
        sc = (cols[0] if len(cols) == 1 else jnp.concatenate(cols, axis=1)) * scale + bias
        m_prev = m_scr[...]
        m_new = jnp.maximum(m_prev, jnp.max(sc, axis=1, keepdims=True))
        alpha = jnp.exp(m_prev - m_new)
        p = jnp.exp(sc - m_new)
        l_scr[...] = alpha * l_scr[...] + jnp.sum(p, axis=1, keepdims=True)
        m_scr[...] = m_new
        pb = p.astype(BF16)
        for h in range(nh):
            rows = slice(h * t, (h + 1) * t)
            o = jnp.zeros((t, dh), F32)
            for i, vr in enumerate(v_refs):
                o = o + jnp.dot(pb[rows, i * page:(i + 1) * page], head_rows(vr, h), preferred_element_type=F32)
            acc_scr[rows, :] = alpha[rows] * acc_scr[rows, :] + o

    bias = jnp.concatenate([jnp.tile(bp_ref[0, p], (nh, 1)) for p in range(pp)], axis=1)
    update(kp, vp, bias)

    @pl.when(s == pl.num_programs(1) - 1)
    def _():
        update([kn_ref.at[0]], [vn_ref.at[0]], jnp.tile(bn_ref[0], (nh, 1)))
        o_ref[0] = (acc_scr[...] / l_scr[...]).astype(o_ref.dtype)


def _attention_sample(page_table, cache_k, cache_v, q_rows, bias_past, bias_new, k_new, v_new, t, nh, dh, pp):
    b, n_pages = page_table.shape
    prow = k_new.shape[1]
    r = nh * t
    page_spec = lambda p: pl.BlockSpec((prow, dh), lambda i, s, pt: (pt[i, s * pp + p], 0))
    per_b = lambda shp: pl.BlockSpec((1,) + shp, lambda i, s, pt: (i,) + (0,) * len(shp))
    kern = functools.partial(_attn_sample_kernel, pp=pp, t=t, nh=nh, dh=dh, scale=dh ** -0.5)
    return pl.pallas_call(
        kern,
        grid_spec=pltpu.PrefetchScalarGridSpec(
            num_scalar_prefetch=1,
            grid=(b, n_pages // pp),
            in_specs=[page_spec(p) for p in range(pp)] * 2 + [
                per_b((r, dh)),
                pl.BlockSpec((1, pp, t, LANE), lambda i, s, pt: (i, s, 0, 0)),
                per_b((t, LANE)), per_b((prow, dh)), per_b((prow, dh))],
            out_specs=per_b((r, dh)),
            scratch_shapes=[pltpu.VMEM((r, 1), F32), pltpu.VMEM((r, 1), F32), pltpu.VMEM((r, dh), F32)]),
        out_shape=jax.ShapeDtypeStruct((b, r, dh), BF16),
        compiler_params=_cparams(("arbitrary", "arbitrary"), 40),
    )(page_table, *([cache_k] * pp), *([cache_v] * pp), q_rows, bias_past, bias_new, k_new, v_new)


def _merge_kernel(ra_ref, wr_ref, at_ref, wa_ref, ga_ref, gb_ref, o_ref):
    a = jnp.dot(ra_ref[...], wr_ref[...].astype(BF16), preferred_element_type=F32)
    b = jnp.dot(at_ref[...], wa_ref[...].astype(BF16), preferred_element_type=F32)
    o_ref[...] = (jax.nn.sigmoid(ga_ref[...]) * a + jax.nn.sigmoid(gb_ref[...]) * b).astype(o_ref.dtype)


def _merge(ra, w_ret_o, att, w_att_o, ga, gb):
    m, kr = ra.shape
    ka = att.shape[1]
    d = w_ret_o.shape[1]
    bm = _pick(m, (528, 512, 256, 128, 64, 32, 16, 8))
    bn = _pick(d, (512, 256, 128))
    row = lambda w: pl.BlockSpec((bm, w), lambda i, j: (i, 0))
    col = lambda kk: pl.BlockSpec((kk, bn), lambda i, j: (0, j))
    tile = pl.BlockSpec((bm, bn), lambda i, j: (i, j))
    return pl.pallas_call(
        _merge_kernel,
        grid=(m // bm, d // bn),
        in_specs=[row(kr), col(kr), row(ka), col(ka), tile, tile],
        out_specs=tile,
        out_shape=jax.ShapeDtypeStruct((m, d), BF16),
        compiler_params=_cparams(("arbitrary", "arbitrary"), 52),
    )(ra, w_ret_o, att, w_att_o, ga, gb)


def _router_kernel(h_ref, w_ref, b_ref, ti_ref, tw_ref, rk_ref, cnt_ref, cnt_scr, *, n_exp):
    s = jax.nn.sigmoid(jnp.dot(h_ref[...].astype(BF16), w_ref[...].astype(BF16), preferred_element_type=F32))
    sb = s + b_ref[...]
    tm = s.shape[0]
    lane = lax.broadcasted_iota(jnp.int32, (tm, n_exp), 1)
    per = n_exp // N_GROUPS
    grp = lane // per
    ninf = -jnp.inf

    def first_argmax(x):
        mx = jnp.max(x, axis=1, keepdims=True)
        idx = jnp.min(jnp.where(x == mx, lane, n_exp), axis=1, keepdims=True)
        return mx, idx

    gscore = []
    for g in range(N_GROUPS):
        xg = jnp.where(grp == g, sb, ninf)
        m1, i1 = first_argmax(xg)
        m2 = jnp.max(jnp.where(lane == i1, ninf, xg), axis=1, keepdims=True)
        gscore.append(m1 + m2)
    keep = jnp.zeros((tm, n_exp), jnp.int32)
    for g in range(N_GROUPS):
        rank = jnp.zeros((tm, 1), jnp.int32)
        for o in range(N_GROUPS):
            if o == g:
                continue
            ahead = (gscore[o] > gscore[g]) if o > g else (gscore[o] >= gscore[g])
            rank = rank + ahead.astype(jnp.int32)
        keep = jnp.where(grp == g, (rank < TOPK_GROUPS).astype(jnp.int32), keep)
    x = jnp.where(keep > 0, sb, ninf)
    olane = lax.broadcasted_iota(jnp.int32, (tm, LANE), 1)
    ti = jnp.zeros((tm, LANE), jnp.int32)
    tw = jnp.zeros((tm, LANE), F32)
    tot = jnp.zeros((tm, 1), F32)
    onehot = jnp.zeros((tm, n_exp), F32)
    picks = []
    for j in range(MOE_TOPK):
        _, idx = first_argmax(x)
        sj = jnp.sum(jnp.where(lane == idx, s, 0.0), axis=1, keepdims=True)
        x = jnp.where(lane == idx, ninf, x)
        onehot = onehot + jnp.where(lane == idx, 1.0, 0.0)
        picks.append(idx)
        ti = jnp.where(olane == j, idx, ti)
        tw = jnp.where(olane == j, sj, tw)
        tot = tot + sj
    ti_ref[...] = ti
    tw_ref[...] = tw / tot * ROUTED_SCALE

    @pl.when(pl.program_id(0) == 0)
    def _():
        cnt_scr[...] = jnp.zeros_like(cnt_scr)
    below = (lax.broadcasted_iota(jnp.int32, (tm, tm), 0) > lax.broadcasted_iota(jnp.int32, (tm, tm), 1))
    ltri = jnp.where(below, 1.0, 0.0).astype(BF16)
    prefix = jnp.dot(ltri, onehot.astype(BF16), preferred_element_type=F32) + cnt_scr[...]
    rk = jnp.zeros((tm, LANE), F32)
    for j in range(MOE_TOPK):
        rj = jnp.sum(jnp.where(lane == picks[j], prefix, 0.0), axis=1, keepdims=True)
        rk = jnp.where(olane == j, rj, rk)
    rk_ref[...] = rk.astype(jnp.int32)
    cnt_scr[...] = cnt_scr[...] + jnp.sum(onehot, axis=0, keepdims=True)
    cnt_ref[...] = cnt_scr[...]


def _route(h, w_router, b_router):
    m, d = h.shape
    n_exp = w_router.shape[1]
    tm = _pick(m, (256, 128, 64, 32, 16, 8))
    return pl.pallas_call(
        functools.partial(_router_kernel, n_exp=n_exp),
        grid=(m // tm,),
        in_specs=[pl.BlockSpec((tm, d), lambda i: (i, 0)),
                  pl.BlockSpec((d, n_exp), lambda i: (0, 0)),
                  pl.BlockSpec((1, n_exp), lambda i: (0, 0))],
        out_specs=[pl.BlockSpec((tm, LANE), lambda i: (i, 0))] * 3 + [pl.BlockSpec((1, n_exp), lambda i: (0, 0))],
        out_shape=[jax.ShapeDtypeStruct((m, LANE), jnp.int32), jax.ShapeDtypeStruct((m, LANE), F32),
                   jax.ShapeDtypeStruct((m, LANE), jnp.int32), jax.ShapeDtypeStruct((1, n_exp), F32)],
        scratch_shapes=[pltpu.VMEM((1, n_exp), F32)],
        compiler_params=_cparams(("arbitrary",), 32),
    )(h, w_router, b_router.reshape(1, n_exp))


def _dest_kernel(ti_ref, rk_ref, base_ref, o_ref, *, n_exp):
    tm = ti_ref.shape[0]
    lane = lax.broadcasted_iota(jnp.int32, (tm, n_exp), 1)
    olane = lax.broadcasted_iota(jnp.int32, (tm, LANE), 1)
    out = jnp.zeros((tm, LANE), jnp.int32)
    for j in range(MOE_TOPK):
        base = jnp.sum(jnp.where(lane == ti_ref[:, j:j + 1], base_ref[...], 0), axis=1, keepdims=True)
        out = jnp.where(olane == j, base + rk_ref[:, j:j + 1], out)
    o_ref[...] = out


def _dest_rows(ti, rk, base):
    m = ti.shape[0]
    n_exp = base.shape[1]
    tm = _pick(m, (256, 128, 64, 32, 16, 8))
    tile = pl.BlockSpec((tm, LANE), lambda i: (i, 0))
    return pl.pallas_call(
        functools.partial(_dest_kernel, n_exp=n_exp),
        grid=(m // tm,),
        in_specs=[tile, tile, pl.BlockSpec((1, n_exp), lambda i: (0, 0))],
        out_specs=tile,
        out_shape=jax.ShapeDtypeStruct((m, LANE), jnp.int32),
        compiler_params=_cparams(("arbitrary",), 32),
    )(ti, rk, base)


def _expert_kernel(be_ref, nu_ref, nv_ref, tok_hbm, h_hbm, wg_ref, wu_ref, wd_ref, y_ref,
                   x_buf, idx_smem, row_sem, idx_sem, *, bm):
    i, hf = pl.program_id(0), pl.program_id(1)
    n_used = nu_ref[0]

    def idx_copy(blk):
        return pltpu.make_async_copy(tok_hbm.at[blk], idx_smem, idx_sem)

    def row_copy(blk_slot, r, tok):
        return pltpu.make_async_copy(h_hbm.at[pl.ds(tok, 1)], x_buf.at[blk_slot, pl.ds(r, 1)],
                                     row_sem.at[blk_slot])

    def row_groups(blk):
        return (nv_ref[blk] + DMA_UNROLL - 1) // DMA_UNROLL

    def start_gather(blk):
        slot = blk % 2

        def body(g, _):
            for k in range(DMA_UNROLL):
                r = g * DMA_UNROLL + k
                row_copy(slot, r, idx_smem[r]).start()
            return 0
        lax.fori_loop(0, row_groups(blk), body, 0)

    def wait_gather(blk):
        slot = blk % 2

        def body(g, _):
            for k in range(DMA_UNROLL):
                row_copy(slot, g * DMA_UNROLL + k, 0).wait()
            return 0
        lax.fori_loop(0, row_groups(blk), body, 0)

    @pl.when(jnp.logical_and(i == 0, hf == 0))
    def _():
        x_buf[...] = jnp.zeros_like(x_buf)
        idx_copy(0).start()
        idx_copy(0).wait()
        start_gather(0)

    @pl.when(jnp.logical_and(i < n_used, hf == 0))
    def _():
        wait_gather(i)

    @pl.when(jnp.logical_and(i + 1 < n_used, hf == 0))
    def _():
        idx_copy(i + 1).start()

    @pl.when(jnp.logical_and(i + 1 < n_used, hf == 1))
    def _():
        idx_copy(i + 1).wait()
        start_gather(i + 1)

    def ffn_half():
        x = x_buf[i % 2].astype(BF16)
        g = jnp.dot(x, wg_ref[0].astype(BF16), preferred_element_type=F32)
        u = jnp.dot(x, wu_ref[0].astype(BF16), preferred_element_type=F32)
        a = (_silu(g) * u).astype(BF16)
        return jnp.dot(a, wd_ref[0].astype(BF16), preferred_element_type=F32)

    @pl.when(jnp.logical_and(i < n_used, hf == 0))
    def _():
        y_ref[...] = ffn_half()

    @pl.when(jnp.logical_and(i < n_used, hf == 1))
    def _():
        y_ref[...] = y_ref[...] + ffn_half()


def _experts(h, block_e, n_used, n_valid, row_tok, w_gate, w_up, w_down, bm):
    m, d = h.shape
    n_exp, _, de = w_gate.shape
    nb = block_e.shape[0]
    deh = de // 2
    last_used = lambda i, nu: jnp.minimum(i, nu[0] - 1)
    return pl.pallas_call(
        functools.partial(_expert_kernel, bm=bm),
        grid_spec=pltpu.PrefetchScalarGridSpec(
            num_scalar_prefetch=3,
            grid=(nb, 2),
            in_specs=[pl.BlockSpec(memory_space=pl.ANY),
                      pl.BlockSpec(memory_space=pl.ANY),
                      pl.BlockSpec((1, d, deh), lambda i, hf, be, nu, nv: (be[i], 0, hf)),
                      pl.BlockSpec((1, d, deh), lambda i, hf, be, nu, nv: (be[i], 0, hf)),
                      pl.BlockSpec((1, deh, d), lambda i, hf, be, nu, nv: (be[i], hf, 0))],
            out_specs=pl.BlockSpec((bm, d), lambda i, hf, be, nu, nv: (last_used(i, nu), 0)),
            scratch_shapes=[pltpu.VMEM((2, bm, d), F32), pltpu.SMEM((bm,), jnp.int32),
                            pltpu.SemaphoreType.DMA((2,)), pltpu.SemaphoreType.DMA(())]),
        out_shape=jax.ShapeDtypeStruct((nb * bm, d), F32),
        compiler_params=_cparams(("arbitrary", "arbitrary"), 56),
    )(block_e, n_used, n_valid, row_tok, h, w_gate, w_up, w_down)


def _dispatch(ti, rk, counts, bm):
    m = ti.shape[0]
    n_exp = counts.shape[1]
    n_assign = m * MOE_TOPK
    nb = -(-n_assign // bm) + n_exp
    nblk = (counts[0].astype(jnp.int32) + bm - 1) // bm
    blk_end = jnp.cumsum(nblk)
    blk_start = blk_end - nblk
    n_used = blk_end[-1]
    dest = _dest_rows(ti, rk, (blk_start * bm)[None, :])[:, :MOE_TOPK]
    tok = jnp.arange(n_assign, dtype=jnp.int32) // MOE_TOPK
    row_tok = jnp.zeros((nb * bm,), jnp.int32).at[dest.reshape(-1)].set(tok, unique_indices=True)
    blk = jnp.arange(nb, dtype=jnp.int32)
    be = jnp.minimum(jnp.searchsorted(blk_end, blk, side='right'), n_exp - 1).astype(jnp.int32)
    n_valid = jnp.clip(counts[0].astype(jnp.int32)[be] - (blk - blk_start[be]) * bm, 0, bm)
    n_valid = jnp.where(blk < n_used, n_valid, 0).astype(jnp.int32)
    be = jnp.where(blk < n_used, be, be[jnp.maximum(n_used - 1, 0)])
    return be, n_used.reshape(1).astype(jnp.int32), n_valid, row_tok.reshape(nb, bm), dest


def _glu_kernel(h_ref, wg_ref, wu_ref, o_ref):
    x = h_ref[...].astype(BF16)
    g = jnp.dot(x, wg_ref[...].astype(BF16), preferred_element_type=F32)
    u = jnp.dot(x, wu_ref[...].astype(BF16), preferred_element_type=F32)
    o_ref[...] = (_silu(g) * u).astype(o_ref.dtype)


def _glu(h, wg, wu):
    m, d = h.shape
    de = wg.shape[1]
    bm = _pick(m, (528, 512, 256, 128, 64, 32, 16, 8))
    bn = _pick(de, (256, 128))
    return pl.pallas_call(
        _glu_kernel,
        grid=(m // bm, de // bn),
        in_specs=[pl.BlockSpec((bm, d), lambda i, j: (i, 0)),
                  pl.BlockSpec((d, bn), lambda i, j: (0, j)),
                  pl.BlockSpec((d, bn), lambda i, j: (0, j))],
        out_specs=pl.BlockSpec((bm, bn), lambda i, j: (i, j)),
        out_shape=jax.ShapeDtypeStruct((m, de), BF16),
        compiler_params=_cparams(("arbitrary", "arbitrary"), 48),
    )(h, wg, wu)


def _combine_kernel(pos_hbm, y_hbm, tw_ref, sh_ref, x_ref, gate_ref, gam_ref, bet_ref, o_ref,
                    buf, idx_smem, row_sem, idx_sem, *, tt):
    b, j = pl.program_id(0), pl.program_id(1)
    step = b * pl.num_programs(1) + j
    n = tt * MOE_TOPK
    cp = pltpu.make_async_copy(pos_hbm.at[step], idx_smem, idx_sem)
    cp.start()
    cp.wait()

    def row_copy(a, src):
        return pltpu.make_async_copy(y_hbm.at[pl.ds(src, 1)], buf.at[a % MOE_TOPK, pl.ds(a // MOE_TOPK, 1)],
                                     row_sem)

    def start(a, _):
        row_copy(a, idx_smem[a]).start()
        return 0
    lax.fori_loop(0, n, start, 0, unroll=DMA_UNROLL)

    for e in range(MOE_TOPK):
        pltpu.make_async_copy(y_hbm.at[pl.ds(0, tt)], buf.at[e], row_sem).wait()

    tw = tw_ref[0]
    routed = buf[0] * tw[:, 0:1]
    for e in range(1, MOE_TOPK):
        routed = routed + buf[e] * tw[:, e:e + 1]
    z = DN_ALPHA * x_ref[0] + gate_ref[0] * (routed + sh_ref[0])
    o_ref[0] = _layer_norm(z) * gam_ref[...] + bet_ref[...]


def _combine(pos, y_sorted, top_w, shared, x1, gate, gamma, beta):
    b, t, d = x1.shape
    tt = _pick(t, (64, 32, 16, 8))
    n = tt * MOE_TOPK
    pos_steps = pos.reshape(-1, n)
    row = lambda w: pl.BlockSpec((1, tt, w), lambda i, j: (i, j, 0))
    mod = pl.BlockSpec((1, 1, d), lambda i, j: (i, 0, 0))
    vec = pl.BlockSpec((1, d), lambda i, j: (0, 0))
    return pl.pallas_call(
        functools.partial(_combine_kernel, tt=tt),
        grid=(b, t // tt),
        in_specs=[pl.BlockSpec(memory_space=pl.ANY), pl.BlockSpec(memory_space=pl.ANY),
                  row(MOE_TOPK), row(d), row(d), mod, vec, vec],
        out_specs=row(d),
        out_shape=jax.ShapeDtypeStruct((b, t, d), F32),
        scratch_shapes=[pltpu.VMEM((MOE_TOPK, tt, d), F32), pltpu.SMEM((n,), jnp.int32),
                        pltpu.SemaphoreType.DMA(()), pltpu.SemaphoreType.DMA(())],
        compiler_params=_cparams(("arbitrary", "arbitrary"), 40),
    )(pos_steps, y_sorted, top_w, shared, x1, gate, gamma.reshape(1, d), beta.reshape(1, d))


def kernel(x_prompt, x_sample, c_prompt, c_sample, cache_k, cache_v, cache_idx_k, state_ret, page_table, w_cond, b_cond, w_in, w_ret_o, w_att_o, w_out, ln_mix_g, ln_mix_b, w_router, b_router, w_exp_gate, w_exp_up, w_exp_down, w_sh_gate, w_sh_up, w_sh_down, ln_ffn_g, ln_ffn_b):
    bp, tp, d = x_prompt.shape
    bs, ts, _ = x_sample.shape
    assert bp == 1, "prompt group kernels are written for a single prompt sequence"
    n_pool, page, ah, dh = cache_k.shape
    idh = cache_idx_k.shape[2]
    rh, dk, dv = state_ret.shape[1:]
    n_pages = page_table.shape[1]
    past = n_pages * page
    n_exp = w_router.shape[1]
    aw, rqk, rv_w = ah * dh, rh * dk, rh * dv
    d_in = w_in.shape[1]
    ih = d_in - (2 * rqk + 2 * rv_w + 3 * aw + idh + 2 * d)
    ih = ih // (idh + 1)
    iq_w = ih * idh
    widths = (rqk, rqk, rv_w, rv_w, aw, aw, aw, iq_w, idh, ih, d, d)
    assert sum(widths) == d_in
    offs = np.concatenate([[0], np.cumsum(widths)]).tolist()
    o_rq, o_rk, o_rv, o_rg, o_aq, o_ak, o_av, o_iq, o_ik, o_iw, o_ga, o_gb = offs[:12]
    mp, ms = bp * tp, bs * ts
    m = mp + ms

    mod = _modulation(jnp.concatenate([c_prompt, c_sample], axis=0), w_cond, b_cond)
    mods_p = [v[:, None, :] for v in jnp.split(mod[:bp], 6, axis=-1)]
    mods_s = [v[:, None, :] for v in jnp.split(mod[bp:], 6, axis=-1)]

    h = jnp.concatenate([_ln_mod(x_prompt, mods_p[1], mods_p[0], BF16).reshape(mp, d),
                         _ln_mod(x_sample, mods_s[1], mods_s[0], BF16).reshape(ms, d)], axis=0)

    pos = jnp.concatenate([jnp.tile(jnp.arange(tp, dtype=jnp.int32), bp),
                           jnp.tile(past + jnp.arange(ts, dtype=jnp.int32), bs)]).astype(F32)
    half = dk // 2
    inv = ROT_BASE ** (-jnp.arange(half, dtype=F32) / half)
    ang = pos[:, None] * inv[None, :]
    cos, sin = jnp.cos(ang), jnp.sin(ang)

    rq, = _proj(h, w_in, o_rq, rqk, [BF16], _ep_rotate(dk, None), extras=(cos, sin))
    rk, = _proj(h, w_in, o_rk, rqk, [BF16], _ep_rotate(dk, dk ** -0.5), extras=(cos, sin))
    rv, = _proj(h, w_in, o_rv, rv_w, [BF16], _ep_store())
    rg, = _proj(h, w_in, o_rg, rv_w, [F32], _ep_store())
    aq, = _proj(h, w_in, o_aq, aw, [BF16], _ep_store())
    ak, ak_b = _proj(h, w_in, o_ak, aw, [F32, BF16], _ep_store())
    av, av_b = _proj(h, w_in, o_av, aw, [F32, BF16], _ep_store())
    iq, = _proj(h, w_in, o_iq, iq_w, [BF16], _ep_store(idh ** -0.5))
    ik, ik_b = _proj(h, w_in, o_ik, idh, [F32, BF16], _ep_store(), bn=LANE)
    iw, = _proj(h, w_in, o_iw, LANE, [F32], _ep_store(ih ** -0.5), bn=LANE)
    w_gates = w_in[:, o_ga:]
    ga, = _proj(h, w_gates, 0, d, [F32], _ep_store())
    gb, = _proj(h, w_gates, d, d, [F32], _ep_store())

    chunk = _pick(tp, (256, 128))
    ra_p, s_p = _retention(rq, rk, rv, rg, jnp.zeros((bp, rh, dk, dv), F32), bp, tp, chunk, chunk, rh, dk, dv)
    pad_s = lambda a: jnp.pad(a[mp:].reshape(bs, ts, -1), ((0, 0), (0, LANE - ts), (0, 0))).reshape(bs * LANE, -1)
    ra_s, s_s = _retention(pad_s(rq), pad_s(rk), pad_s(rv), pad_s(rg), state_ret.astype(F32),
                           bs, LANE, LANE, ts, rh, dk, dv)
    ra_s = ra_s.reshape(bs, LANE, rv_w)[:, :ts]

    tq = _pick(tp, (256, 128))
    tk_sel = _pick(tp, (2048, 1024, 512, 256, 128))
    tk_att = _pick(tk_sel, (1024, 512, 256, 128))
    bias_p = _select_prompt(iq, ik_b[:mp], iw, tp, ih, idh, min(TOPK_MAX, tp // 4), tq, tk_sel)
    att_p = _attention_prompt(aq, ak_b, av_b, bias_p, tp, ah, dh, tq, tk_att)

    pp = _pick(n_pages, (4, 2, 1))
    hm = lambda a, nh: a.reshape(bs, ts, nh, -1).transpose(0, 2, 1, 3).reshape(bs, nh * ts, -1)
    iq_rows = hm(iq[mp:], ih)
    iw_rows = iw[mp:, :ih].reshape(bs, ts, ih).transpose(0, 2, 1).reshape(bs, ih * ts, 1)
    pad_new = lambda a: jnp.pad(a[mp:].reshape(bs, ts, -1), ((0, 0), (0, LANE - ts), (0, 0)))
    bias_past, bias_new = _select_sample(page_table, cache_idx_k, iq_rows, iw_rows, pad_new(ik_b),
                                         ts, ih, min(TOPK_MAX, (past + ts) // 4), _pick(n_pages, (16, 8, 4, 2, 1)))
    flat_kv = lambda c: c.reshape(n_pool * page * ah, dh)
    new_kv = lambda a: jnp.pad(a[mp:].reshape(bs, ts * ah, dh), ((0, 0), (0, (page - ts) * ah), (0, 0)))
    att_s = _attention_sample(page_table, flat_kv(cache_k), flat_kv(cache_v), hm(aq[mp:], ah).astype(F32),
                              bias_past, bias_new, new_kv(ak), new_kv(av), ts, ah, dh, pp)
    att_s = att_s.reshape(bs, ah, ts, dh).transpose(0, 2, 1, 3)

    ra = jnp.concatenate([ra_p, ra_s.reshape(ms, rv_w)], axis=0)
    att = jnp.concatenate([att_p, att_s.reshape(ms, aw)], axis=0)
    u = _merge(ra, w_ret_o, att, w_att_o, ga, gb)
    mix, = _proj(u, w_out, 0, d, [F32], _ep_store())
    x1_p, h2_p = _post_ln(x_prompt, mix.reshape(1, m, d), mods_p[2], ln_mix_g, ln_mix_b, mods_p[4], mods_p[3])
    x1_s, h2_s = _post_ln(x_sample, mix[mp:].reshape(bs, ts, d), mods_s[2], ln_mix_g, ln_mix_b, mods_s[4], mods_s[3])
    h2 = jnp.concatenate([h2_p.reshape(mp, d), h2_s.reshape(ms, d)], axis=0)

    top_i, top_w, rank, counts = _route(h2, w_router, b_router)
    top_w = top_w[:, :MOE_TOPK]
    n_assign = m * MOE_TOPK
    bm = 320 if n_assign // n_exp >= 256 else _pick(max(n_assign // n_exp, 8), (128, 64, 32, 16, 8))
    block_e, n_used, n_valid, row_tok, dest = _dispatch(top_i, rank, counts, bm)
    y_sorted = _experts(h2, block_e, n_used, n_valid, row_tok, w_exp_gate, w_exp_up, w_exp_down, bm)
    shared, = _proj(_glu(h2, w_sh_gate, w_sh_up), w_sh_down, 0, d, [F32], _ep_store())

    y_p = _combine(dest[:mp], y_sorted, top_w[:mp].reshape(bp, tp, MOE_TOPK), shared.reshape(1, m, d),
                   x1_p, mods_p[5], ln_ffn_g, ln_ffn_b)
    y_s = _combine(dest[mp:], y_sorted, top_w[mp:].reshape(bs, ts, MOE_TOPK), shared[mp:].reshape(bs, ts, d),
                   x1_s, mods_s[5], ln_ffn_g, ln_ffn_b)

    return (y_p, y_s,
            ak[:mp].reshape(bp, tp, ah, dh), av[:mp].reshape(bp, tp, ah, dh), ik[:mp].reshape(bp, tp, idh),
            s_p.reshape(bp, rh, dk, dv),
            ak[mp:].reshape(bs, ts, ah, dh), av[mp:].reshape(bs, ts, ah, dh), ik[mp:].reshape(bs, ts, idh),
            s_s)
```

```python
import functools
import math

import jax
import jax.numpy as jnp
import numpy as np
from jax import lax
from jax.experimental import pallas as pl
from jax.experimental.pallas import tpu as pltpu

ROT_BASE = 10000.0
TOPK_MAX = 256
Q_BLOCK = 128
MOE_TOPK = 8
N_GROUPS = 8
TOPK_GROUPS = 4
ROUTED_SCALE = 2.5
LN_EPS = 1e-5
DEPTH = 1
DN_ALPHA = (2.0 * DEPTH) ** 0.25

LANE = 128
SUBLANE = 8
VMEM_BYTES_V7X = 64 * 2 ** 20

DMA_UNROLL = 8

NEG = -1e30
INT_MIN = -2 ** 31
BF16 = jnp.bfloat16
F32 = jnp.float32

_NT = (((1,), (1,)), ((), ()))
_TN = (((0,), (0,)), ((), ()))


def _cparams(sem, vmem_mb):
    assert vmem_mb * 2 ** 20 < VMEM_BYTES_V7X
    return pltpu.CompilerParams(dimension_semantics=sem, vmem_limit_bytes=vmem_mb * 2 ** 20)


def _pick(n, cands):
    for c in cands:
        if c <= n and n % c == 0:
            return c
    return n


def _silu(x):
    return x * jax.nn.sigmoid(x)


def _layer_norm(x):
    mu = jnp.mean(x, axis=-1, keepdims=True)
    xc = x - mu
    var = jnp.mean(xc * xc, axis=-1, keepdims=True)
    return xc * lax.rsqrt(var + LN_EPS)


def _cond_kernel(c_ref, w_ref, b_ref, o_ref):
    a = _silu(c_ref[...]).astype(BF16)
    o_ref[...] = jnp.dot(a, w_ref[...].astype(BF16), preferred_element_type=F32) + b_ref[...]


def _modulation(c, w_cond, b_cond):
    nb, d = c.shape
    n = w_cond.shape[1]
    mp = -(-nb // SUBLANE) * SUBLANE
    cp = jnp.pad(c, ((0, mp - nb), (0, 0)))
    tn = _pick(n, (512, 256, 128))
    out = pl.pallas_call(
        _cond_kernel,
        grid=(n // tn,),
        in_specs=[pl.BlockSpec((mp, d), lambda j: (0, 0)),
                  pl.BlockSpec((d, tn), lambda j: (0, j)),
                  pl.BlockSpec((1, tn), lambda j: (0, j))],
        out_specs=pl.BlockSpec((mp, tn), lambda j: (0, j)),
        out_shape=jax.ShapeDtypeStruct((mp, n), F32),
        compiler_params=_cparams(("arbitrary",), 40),
    )(cp, w_cond, b_cond.reshape(1, n))
    return out[:nb]


def _ln_mod_kernel(x_ref, sc_ref, sh_ref, o_ref):
    y = _layer_norm(x_ref[0])
    o_ref[0] = (y * (1.0 + sc_ref[0]) + sh_ref[0]).astype(o_ref.dtype)


def _ln_mod(x, scale, shift, out_dtype):
    b, t, d = x.shape
    tt = _pick(t, (256, 128, 64, 32, 16, 8))
    mod = pl.BlockSpec((1, 1, d), lambda i, j: (i, 0, 0))
    row = pl.BlockSpec((1, tt, d), lambda i, j: (i, j, 0))
    return pl.pallas_call(
        _ln_mod_kernel,
        grid=(b, t // tt),
        in_specs=[row, mod, mod],
        out_specs=row,
        out_shape=jax.ShapeDtypeStruct((b, t, d), out_dtype),
        compiler_params=_cparams(("arbitrary", "arbitrary"), 40),
    )(x, scale, shift)


def _post_ln_kernel(x_ref, y_ref, gate_ref, gam_ref, bet_ref, sc_ref, sh_ref, x1_ref, h_ref):
    z = DN_ALPHA * x_ref[0] + gate_ref[0] * y_ref[0]
    x1 = _layer_norm(z) * gam_ref[...] + bet_ref[...]
    x1_ref[0] = x1
    h_ref[0] = (_layer_norm(x1) * (1.0 + sc_ref[0]) + sh_ref[0]).astype(h_ref.dtype)


def _post_ln(x, y, gate, gamma, beta, scale, shift):
    b, t, d = x.shape
    tt = _pick(t, (256, 128, 64, 32, 16, 8))
    mod = pl.BlockSpec((1, 1, d), lambda i, j: (i, 0, 0))
    row = pl.BlockSpec((1, tt, d), lambda i, j: (i, j, 0))
    vec = pl.BlockSpec((1, d), lambda i, j: (0, 0))
    return pl.pallas_call(
        _post_ln_kernel,
        grid=(b, t // tt),
        in_specs=[row, row, mod, vec, vec, mod, mod],
        out_specs=[row, row],
        out_shape=[jax.ShapeDtypeStruct((b, t, d), F32), jax.ShapeDtypeStruct((b, t, d), F32)],
        compiler_params=_cparams(("arbitrary", "arbitrary"), 48),
    )(x, y, gate, gamma.reshape(1, d), beta.reshape(1, d), scale, shift)


def _proj_kernel(*refs, n_extra, n_out, epilogue):
    a_ref, w_ref = refs[0], refs[1]
    extra = refs[2:2 + n_extra]
    outs = refs[2 + n_extra:2 + n_extra + n_out]
    acc = jnp.dot(a_ref[...], w_ref[...].astype(BF16), preferred_element_type=F32)
    epilogue(acc, extra, outs)


def _proj(a, w, col0, ncols, out_dtypes, epilogue, extras=(), bm=None, bn=None, vmem_mb=52):
    m, k = a.shape
    bm = bm or _pick(m, (1056, 1024, 768, 512, 256, 128, 64, 32, 16, 8))
    bn = bn or _pick(math.gcd(ncols, col0) if col0 else ncols, (512, 256, 128))
    assert col0 % bn == 0 and ncols % bn == 0 and m % bm == 0
    cb = col0 // bn
    in_specs = [pl.BlockSpec((bm, k), lambda i, j: (i, 0)),
                pl.BlockSpec((k, bn), lambda i, j: (0, cb + j))]
    for e in extras:
        in_specs.append(pl.BlockSpec((bm, e.shape[1]), lambda i, j: (i, 0)))
    out_specs = [pl.BlockSpec((bm, bn), lambda i, j: (i, j)) for _ in out_dtypes]
    out_shape = [jax.ShapeDtypeStruct((m, ncols), dt) for dt in out_dtypes]
    kern = functools.partial(_proj_kernel, n_extra=len(extras), n_out=len(out_dtypes), epilogue=epilogue)
    return pl.pallas_call(
        kern,
        grid=(m // bm, ncols // bn),
        in_specs=in_specs,
        out_specs=out_specs,
        out_shape=out_shape,
        compiler_params=_cparams(("arbitrary", "arbitrary"), vmem_mb),
    )(a, w, *extras)


def _ep_store(scale=None):
    def ep(acc, extra, outs):
        v = acc if scale is None else acc * scale
        for o in outs:
            o[...] = v.astype(o.dtype)
    return ep


def _ep_rotate(dk, scale):
    half = dk // 2

    def ep(acc, extra, outs):
        cos, sin = extra[0][...], extra[1][...]
        (o,) = outs
        for h in range(acc.shape[1] // dk):
            x1 = acc[:, h * dk:h * dk + half]
            x2 = acc[:, h * dk + half:(h + 1) * dk]
            r1 = x1 * cos - x2 * sin
            r2 = x2 * cos + x1 * sin
            if scale is not None:
                r1, r2 = r1 * scale, r2 * scale
            o[:, h * dk:h * dk + half] = r1.astype(o.dtype)
            o[:, h * dk + half:(h + 1) * dk] = r2.astype(o.dtype)
    return ep


def _ret_kernel(q_ref, k_ref, v_ref, g_ref, s0_ref, intra_ref, qd_ref, kd_ref, cd_ref,
                o_ref, s_out_ref, s_scr):
    c = pl.program_id(2)

    @pl.when(c == 0)
    def _():
        s_scr[...] = s0_ref[0, 0]

    q, k, v = q_ref[...], k_ref[...], v_ref[...]
    s = s_scr[...]
    att = lax.dot_general(q, k, _NT, preferred_element_type=F32) * intra_ref[0]
    o = jnp.dot(att.astype(BF16), v, preferred_element_type=F32)
    o = o + jnp.dot((q.astype(F32) * qd_ref[0]).astype(BF16), s.astype(BF16), preferred_element_type=F32)
    kd = (k.astype(F32) * kd_ref[0]).astype(BF16)
    s_new = s * cd_ref[0] + lax.dot_general(kd, v, _TN, preferred_element_type=F32)
    s_scr[...] = s_new
    o_ref[...] = (_silu(g_ref[...]) * _layer_norm(o)).astype(o_ref.dtype)

    @pl.when(c == pl.num_programs(2) - 1)
    def _():
        s_out_ref[0, 0] = s_new


def _ret_tables(nh, chunk, n_valid, dv):
    lg = jnp.log1p(-jnp.exp2(-5.0 - jnp.arange(nh, dtype=F32)))
    idx = jnp.arange(chunk, dtype=F32)
    diff = idx[:, None] - idx[None, :]
    intra = jnp.where(diff >= 0, jnp.exp(lg[:, None, None] * jnp.maximum(diff, 0.0)), 0.0)
    q_dec = jnp.exp(lg[:, None] * (idx[None, :] + 1.0))[:, :, None]
    k_dec = jnp.where(idx[None, :] < n_valid, jnp.exp(lg[:, None] * jnp.maximum(n_valid - 1.0 - idx[None, :], 0.0)),
                      0.0)[:, :, None]
    c_dec = jnp.broadcast_to(jnp.exp(lg * n_valid)[:, None, None], (nh, 1, dv))
    return intra, q_dec, k_dec, c_dec


def _retention(q, k, v, g, s0, nb, t, chunk, n_valid, nh, dk, dv):
    nc = t // chunk
    intra, q_dec, k_dec, c_dec = _ret_tables(nh, chunk, n_valid, dv)
    tab = lambda shp: pl.BlockSpec((1,) + shp, lambda b, h, c: (h, 0, 0))
    row = lambda w: pl.BlockSpec((chunk, w), lambda b, h, c: (b * nc + c, h))
    st = pl.BlockSpec((1, 1, dk, dv), lambda b, h, c: (b, h, 0, 0))
    return pl.pallas_call(
        _ret_kernel,
        grid=(nb, nh, nc),
        in_specs=[row(dk), row(dk), row(dv), row(dv), st,
                  tab((chunk, chunk)), tab((chunk, 1)), tab((chunk, 1)), tab((1, dv))],
        out_specs=[row(dv), st],
        out_shape=[jax.ShapeDtypeStruct((nb * t, nh * dv), BF16),
                   jax.ShapeDtypeStruct((nb, nh, dk, dv), F32)],
        scratch_shapes=[pltpu.VMEM((dk, dv), F32)],
        compiler_params=_cparams(("arbitrary", "arbitrary", "arbitrary"), 32),
    )(q, k, v, g, s0, intra, q_dec, k_dec, c_dec)


def _to_key(score):
    bits = pltpu.bitcast(score + 0.0, jnp.int32)
    return jnp.where(bits >= 0, bits, bits ^ jnp.int32(0x7FFFFFFF))


def _fold_lanes(x):
    acc = x[:, :LANE]
    for g in range(1, x.shape[1] // LANE):
        acc = acc + x[:, g * LANE:(g + 1) * LANE]
    return acc


def _kth_largest_key(key_scr, n_tiles, rows, n_sel):
    def count_ge(cand):
        def body(kt, cnt):
            return cnt + _fold_lanes((key_scr[kt] >= cand).astype(jnp.int32))
        cnt = lax.fori_loop(0, n_tiles, body, jnp.zeros((rows, LANE), jnp.int32))
        return jnp.sum(cnt, axis=1, keepdims=True)

    def bit_step(b, t):
        cand = t + jnp.left_shift(jnp.int32(1), 31 - b)
        return jnp.where(count_ge(cand) >= n_sel, cand, t)

    return lax.fori_loop(0, 32, bit_step, jnp.full((rows, 1), INT_MIN, jnp.int32))


def _tie_cutoff(key_scr, n_tiles, rows, w, n_sel, thr):
    def counts(kt, c):
        k = key_scr[kt]
        gt = c[0] + _fold_lanes((k > thr).astype(jnp.int32))
        eq = c[1] + _fold_lanes((k == thr).astype(jnp.int32))
        return gt, eq
    z = jnp.zeros((rows, LANE), jnp.int32)
    gt, eq = lax.fori_loop(0, n_tiles, counts, (z, z))
    need = n_sel - jnp.sum(gt, axis=1, keepdims=True)
    n_eq = jnp.sum(eq, axis=1, keepdims=True)

    def search(_):
        def count_eq_le(cut):
            def body(kt, cnt):
                col = kt * w + lax.broadcasted_iota(jnp.int32, (rows, w), 1)
                hit = jnp.where(col <= cut, (key_scr[kt] == thr).astype(jnp.int32), 0)
                return cnt + _fold_lanes(hit)
            cnt = lax.fori_loop(0, n_tiles, body, jnp.zeros((rows, LANE), jnp.int32))
            return jnp.sum(cnt, axis=1, keepdims=True)

        def bit_step(b, cut):
            cand = cut - jnp.left_shift(jnp.int32(1), 30 - b)
            ok = jnp.logical_and(cand >= 0, count_eq_le(cand) >= need)
            return jnp.where(ok, cand, cut)
        return lax.fori_loop(0, 31, bit_step, jnp.full((rows, 1), 2 ** 31 - 1, jnp.int32))

    excess = jnp.max(jnp.where(n_eq > need, 1, 0)) > 0
    return lax.cond(excess, search, lambda _: jnp.full((rows, 1), 2 ** 31 - 1, jnp.int32), 0)


def _select_bias(key, thr, cut, col, valid):
    bias = jnp.where(key > thr, 0.0, jnp.where(key == thr, jnp.where(col <= cut, 0.0, NEG), NEG))
    return bias if valid is None else jnp.where(valid, bias, NEG)


def _sel_prompt_kernel(iq_ref, ik_ref, iw_ref, bias_ref, key_scr, *, tq, tk, ih, idh, n_sel):
    qi = pl.program_id(0)
    nk_total = bias_ref.shape[1]
    n_tiles = ((qi + 1) * tq + tk - 1) // tk
    row = qi * tq + lax.broadcasted_iota(jnp.int32, (tq, tk), 0)
    lane = lax.broadcasted_iota(jnp.int32, (tq, tk), 1)

    def score_tile(kt, _):
        ik = ik_ref[pl.ds(pl.multiple_of(kt * tk, tk), tk), :]
        acc = jnp.zeros((tq, tk), F32)
        for h in range(ih):
            r = lax.dot_general(iq_ref[:, h * idh:(h + 1) * idh], ik, _NT, preferred_element_type=F32)
            acc = acc + jnp.maximum(r, 0.0) * iw_ref[:, h:h + 1]
        key_scr[kt] = jnp.where(kt * tk + lane <= row, _to_key(acc), INT_MIN)
        return 0
    lax.fori_loop(0, n_tiles, score_tile, 0)

    thr = _kth_largest_key(key_scr, n_tiles, tq, n_sel)
    cut = _tie_cutoff(key_scr, n_tiles, tq, tk, n_sel, thr)

    def emit(kt, _):
        col = kt * tk + lane
        bias_ref[0, kt] = _select_bias(key_scr[kt], thr, cut, col, col <= row)
        return 0
    lax.fori_loop(0, n_tiles, emit, 0)

    def fill(kt, _):
        bias_ref[0, kt] = jnp.full((tq, tk), NEG, F32)
        return 0
    lax.fori_loop(n_tiles, nk_total, fill, 0)


def _select_prompt(iq, ik, iw, t, ih, idh, n_sel, tq, tk):
    nq, nk = t // tq, t // tk
    kern = functools.partial(_sel_prompt_kernel, tq=tq, tk=tk, ih=ih, idh=idh, n_sel=n_sel)
    return pl.pallas_call(
        kern,
        grid=(nq,),
        in_specs=[pl.BlockSpec((tq, ih * idh), lambda i: (i, 0)),
                  pl.BlockSpec((t, idh), lambda i: (0, 0)),
                  pl.BlockSpec((tq, iw.shape[1]), lambda i: (i, 0))],
        out_specs=pl.BlockSpec((1, nk, tq, tk), lambda i: (i, 0, 0, 0)),
        out_shape=jax.ShapeDtypeStruct((nq, nk, tq, tk), F32),
        scratch_shapes=[pltpu.VMEM((nk, tq, tk), jnp.int32)],
        compiler_params=_cparams(("arbitrary",), 48),
    )(iq, ik, iw)


def _attn_prompt_kernel(q_ref, k_ref, v_ref, b_ref, o_ref, m_scr, l_scr, acc_scr, *, tq, tk, nh, dh, scale):
    i, j = pl.program_id(0), pl.program_id(1)
    last = ((i + 1) * tq - 1) // tk

    @pl.when(j == 0)
    def _():
        m_scr[...] = jnp.full(m_scr.shape, -jnp.inf, F32)
        l_scr[...] = jnp.zeros_like(l_scr)
        acc_scr[...] = jnp.zeros_like(acc_scr)

    @pl.when(j <= last)
    def _():
        nsub = b_ref.shape[0]
        bias = b_ref[0, 0] if nsub == 1 else jnp.concatenate([b_ref[a, 0] for a in range(nsub)], axis=0)
        grp = 4 if nh % 4 == 0 else 1
        for h0 in range(0, nh, grp):
            hs = range(h0, h0 + grp)
            sl = {h: slice(h * dh, (h + 1) * dh) for h in hs}
            s = {h: lax.dot_general(q_ref[:, sl[h]], k_ref[:, sl[h]], _NT, preferred_element_type=F32) * scale + bias
                 for h in hs}
            m_prev = {h: m_scr[h] for h in hs}
            m_new = {h: jnp.maximum(m_prev[h], jnp.max(s[h], axis=1, keepdims=True)) for h in hs}
            p = {h: jnp.exp(s[h] - m_new[h]) for h in hs}
            alpha = {h: jnp.exp(m_prev[h] - m_new[h]) for h in hs}
            pv = {h: jnp.dot(p[h].astype(BF16), v_ref[:, sl[h]], preferred_element_type=F32) for h in hs}
            for h in hs:
                l_scr[h] = alpha[h] * l_scr[h] + jnp.sum(p[h], axis=1, keepdims=True)
                acc_scr[:, sl[h]] = alpha[h] * acc_scr[:, sl[h]] + pv[h]
                m_scr[h] = m_new[h]

    @pl.when(j == last)
    def _():
        for h in range(nh):
            sl = slice(h * dh, (h + 1) * dh)
            o_ref[:, sl] = (acc_scr[:, sl] / l_scr[h]).astype(o_ref.dtype)


def _attention_prompt(q, k, v, bias, t, nh, dh, tq, tk):
    nq, nk = t // tq, t // tk
    nsub = tq // bias.shape[2]
    kdiv = bias.shape[3] // tk
    last = lambda i: ((i + 1) * tq - 1) // tk
    kcl = lambda i, j: jnp.minimum(j, last(i))
    kern = functools.partial(_attn_prompt_kernel, tq=tq, tk=tk, nh=nh, dh=dh, scale=dh ** -0.5)
    return pl.pallas_call(
        kern,
        grid=(nq, nk),
        in_specs=[pl.BlockSpec((tq, nh * dh), lambda i, j: (i, 0)),
                  pl.BlockSpec((tk, nh * dh), lambda i, j: (jnp.minimum(j, last(i)), 0)),
                  pl.BlockSpec((tk, nh * dh), lambda i, j: (jnp.minimum(j, last(i)), 0)),
                  pl.BlockSpec((nsub, 1, tq // nsub, tk),
                               lambda i, j: (i, kcl(i, j) // kdiv, 0, kcl(i, j) % kdiv))],
        out_specs=pl.BlockSpec((tq, nh * dh), lambda i, j: (i, 0)),
        out_shape=jax.ShapeDtypeStruct((t, nh * dh), BF16),
        scratch_shapes=[pltpu.VMEM((nh, tq, 1), F32), pltpu.VMEM((nh, tq, 1), F32),
                        pltpu.VMEM((tq, nh * dh), F32)],
        compiler_params=_cparams(("arbitrary", "arbitrary"), 48),
    )(q, k, v, bias)


def _sel_sample_kernel(pt_ref, *refs, pp, t, ih, n_sel):
    pages = refs[:pp]
    iq_ref, iw_ref, ikn_ref, bp_ref, bn_ref, key_scr = refs[pp:]
    s = pl.program_id(1)
    n_past = bp_ref.shape[1]
    iq = iq_ref[0]
    iw = iw_ref[0]

    def scores(ik):
        r = lax.dot_general(iq, ik, _NT, preferred_element_type=F32)
        r = jnp.maximum(r, 0.0) * iw
        acc = r[0:t]
        for h in range(1, ih):
            acc = acc + r[h * t:(h + 1) * t]
        return acc

    for p in range(pp):
        key_scr[s * pp + p] = _to_key(scores(pages[p][0].astype(BF16)))

    @pl.when(s == pl.num_programs(1) - 1)
    def _():
        trow = lax.broadcasted_iota(jnp.int32, (t, LANE), 0)
        lane = lax.broadcasted_iota(jnp.int32, (t, LANE), 1)
        new_ok = lane <= trow
        key_scr[n_past] = jnp.where(new_ok, _to_key(scores(ikn_ref[0])), INT_MIN)
        thr = _kth_largest_key(key_scr, n_past + 1, t, n_sel)
        cut = _tie_cutoff(key_scr, n_past + 1, t, LANE, n_sel, thr)

        def emit(kt, _):
            bp_ref[0, kt] = _select_bias(key_scr[kt], thr, cut, kt * LANE + lane, None)
            return 0
        lax.fori_loop(0, n_past, emit, 0)
        bn_ref[0] = _select_bias(key_scr[n_past], thr, cut, n_past * LANE + lane, new_ok)


def _select_sample(page_table, cache_idx_k, iq_rows, iw_rows, ik_new, t, ih, n_sel, pp):
    b, n_pages = page_table.shape
    _, page, idh = cache_idx_k.shape
    assert page == LANE and t <= LANE
    r = ih * t
    page_spec = lambda p: pl.BlockSpec((1, page, idh), lambda i, s, pt: (pt[i, s * pp + p], 0, 0))
    per_b = lambda shp: pl.BlockSpec((1,) + shp, lambda i, s, pt: (i,) + (0,) * len(shp))
    kern = functools.partial(_sel_sample_kernel, pp=pp, t=t, ih=ih, n_sel=n_sel)
    return pl.pallas_call(
        kern,
        grid_spec=pltpu.PrefetchScalarGridSpec(
            num_scalar_prefetch=1,
            grid=(b, n_pages // pp),
            in_specs=[page_spec(p) for p in range(pp)] + [per_b((r, idh)), per_b((r, 1)), per_b((LANE, idh))],
            out_specs=[per_b((n_pages, t, LANE)), per_b((t, LANE))],
            scratch_shapes=[pltpu.VMEM((n_pages + 1, t, LANE), jnp.int32)]),
        out_shape=[jax.ShapeDtypeStruct((b, n_pages, t, LANE), F32),
                   jax.ShapeDtypeStruct((b, t, LANE), F32)],
        compiler_params=_cparams(("arbitrary", "arbitrary"), 32),
    )(page_table, *([cache_idx_k] * pp), iq_rows, iw_rows, ik_new)


def _attn_sample_kernel(pt_ref, *refs, pp, t, nh, dh, scale):
    kp, vp = refs[:pp], refs[pp:2 * pp]
    q_ref, bp_ref, bn_ref, kn_ref, vn_ref, o_ref, m_scr, l_scr, acc_scr = refs[2 * pp:]
    s = pl.program_id(1)
    page = kp[0].shape[0] // nh

    @pl.when(s == 0)
    def _():
        m_scr[...] = jnp.full(m_scr.shape, -jnp.inf, F32)
        l_scr[...] = jnp.zeros_like(l_scr)
        acc_scr[...] = jnp.zeros_like(acc_scr)

    def head_rows(ref, h):
        return ref[pl.ds(h, page, stride=nh), :].astype(BF16)

    def update(k_refs, v_refs, bias):
        cols = []
        for kr in k_refs:
            parts = [lax.dot_general(q_ref[0, h * t:(h + 1) * t, :].astype(BF16), head_rows(kr, h), _NT,
                                     preferred_element_type=F32) for h in range(nh)]
            cols.append(jnp.concatenate(parts, axis=0))
        sc = (cols[0] if len(cols) == 1 else jnp.concatenate(cols, axis=1)) * scale + bias
        m_prev = m_scr[...]
        m_new = jnp.maximum(m_prev, jnp.max(sc, axis=1, keepdims=True))
        alpha = jnp.exp(m_prev - m_new)
        p = jnp.exp(sc - m_new)
        l_scr[...] = alpha * l_scr[...] + jnp.sum(p, axis=1, keepdims=True)
        m_scr[...] = m_new
        pb = p.astype(BF16)
        for h in range(nh):
            rows = slice(h * t, (h + 1) * t)
            o = jnp.zeros((t, dh), F32)
            for i, vr in enumerate(v_refs):
                o = o + jnp.dot(pb[rows, i * page:(i + 1) * page], head_rows(vr, h), preferred_element_type=F32)
            acc_scr[rows, :] = alpha[rows] * acc_scr[rows, :] + o

    bias = jnp.concatenate([jnp.tile(bp_ref[0, p], (nh, 1)) for p in range(pp)], axis=1)
    update(kp, vp, bias)

    @pl.when(s == pl.num_programs(1) - 1)
    def _():
        update([kn_ref.at[0]], [vn_ref.at[0]], jnp.tile(bn_ref[0], (nh, 1)))
        o_ref[0] = (acc_scr[...] / l_scr[...]).astype(o_ref.dtype)


def _attention_sample(page_table, cache_k, cache_v, q_rows, bias_past, bias_new, k_new, v_new, t, nh, dh, pp):
    b, n_pages = page_table.shape
    prow = k_new.shape[1]
    r = nh * t
    page_spec = lambda p: pl.BlockSpec((prow, dh), lambda i, s, pt: (pt[i, s * pp + p], 0))
    per_b = lambda shp: pl.BlockSpec((1,) + shp, lambda i, s, pt: (i,) + (0,) * len(shp))
    kern = functools.partial(_attn_sample_kernel, pp=pp, t=t, nh=nh, dh=dh, scale=dh ** -0.5)
    return pl.pallas_call(
        kern,
        grid_spec=pltpu.PrefetchScalarGridSpec(
            num_scalar_prefetch=1,
            grid=(b, n_pages // pp),
            in_specs=[page_spec(p) for p in range(pp)] * 2 + [
                per_b((r, dh)),
                pl.BlockSpec((1, pp, t, LANE), lambda i, s, pt: (i, s, 0, 0)),
                per_b((t, LANE)), per_b((prow, dh)), per_b((prow, dh))],
            out_specs=per_b((r, dh)),
            scratch_shapes=[pltpu.VMEM((r, 1), F32), pltpu.VMEM((r, 1), F32), pltpu.VMEM((r, dh), F32)]),
        out_shape=jax.ShapeDtypeStruct((b, r, dh), BF16),
        compiler_params=_cparams(("arbitrary", "arbitrary"), 40),
    )(page_table, *([cache_k] * pp), *([cache_v] * pp), q_rows, bias_past, bias_new, k_new, v_new)


def _merge_kernel(ra_ref, wr_ref, at_ref, wa_ref, ga_ref, gb_ref, o_ref):
    a = jnp.dot(ra_ref[...], wr_ref[...].astype(BF16), preferred_element_type=F32)
    b = jnp.dot(at_ref[...], wa_ref[...].astype(BF16), preferred_element_type=F32)
    o_ref[...] = (jax.nn.sigmoid(ga_ref[...]) * a + jax.nn.sigmoid(gb_ref[...]) * b).astype(o_ref.dtype)


def _merge(ra, w_ret_o, att, w_att_o, ga, gb):
    m, kr = ra.shape
    ka = att.shape[1]
    d = w_ret_o.shape[1]
    bm = _pick(m, (768, 528, 512, 256, 128, 64, 32, 16, 8))
    bn = _pick(d, (512, 256, 128))
    row = lambda w: pl.BlockSpec((bm, w), lambda i, j: (i, 0))
    col = lambda kk: pl.BlockSpec((kk, bn), lambda i, j: (0, j))
    tile = pl.BlockSpec((bm, bn), lambda i, j: (i, j))
    return pl.pallas_call(
        _merge_kernel,
        grid=(m // bm, d // bn),
        in_specs=[row(kr), col(kr), row(ka), col(ka), tile, tile],
        out_specs=tile,
        out_shape=jax.ShapeDtypeStruct((m, d), BF16),
        compiler_params=_cparams(("arbitrary", "arbitrary"), 54),
    )(ra, w_ret_o, att, w_att_o, ga, gb)


def _router_kernel(h_ref, w_ref, b_ref, ti_ref, tw_ref, rk_ref, cnt_ref, cnt_scr, *, n_exp):
    s = jax.nn.sigmoid(jnp.dot(h_ref[...].astype(BF16), w_ref[...].astype(BF16), preferred_element_type=F32))
    sb = s + b_ref[...]
    tm = s.shape[0]
    lane = lax.broadcasted_iota(jnp.int32, (tm, n_exp), 1)
    per = n_exp // N_GROUPS
    grp = lane // per
    ninf = -jnp.inf

    def first_argmax(x):
        mx = jnp.max(x, axis=1, keepdims=True)
        idx = jnp.min(jnp.where(x == mx, lane, n_exp), axis=1, keepdims=True)
        return mx, idx

    gscore = []
    for g in range(N_GROUPS):
        xg = jnp.where(grp == g, sb, ninf)
        m1, i1 = first_argmax(xg)
        m2 = jnp.max(jnp.where(lane == i1, ninf, xg), axis=1, keepdims=True)
        gscore.append(m1 + m2)
    keep = jnp.zeros((tm, n_exp), jnp.int32)
    for g in range(N_GROUPS):
        rank = jnp.zeros((tm, 1), jnp.int32)
        for o in range(N_GROUPS):
            if o == g:
                continue
            ahead = (gscore[o] > gscore[g]) if o > g else (gscore[o] >= gscore[g])
            rank = rank + ahead.astype(jnp.int32)
        keep = jnp.where(grp == g, (rank < TOPK_GROUPS).astype(jnp.int32), keep)
    x = jnp.where(keep > 0, sb, ninf)
    olane = lax.broadcasted_iota(jnp.int32, (tm, LANE), 1)
    ti = jnp.zeros((tm, LANE), jnp.int32)
    tw = jnp.zeros((tm, LANE), F32)
    tot = jnp.zeros((tm, 1), F32)
    onehot = jnp.zeros((tm, n_exp), F32)
    picks = []
    for j in range(MOE_TOPK):
        _, idx = first_argmax(x)
        sj = jnp.sum(jnp.where(lane == idx, s, 0.0), axis=1, keepdims=True)
        x = jnp.where(lane == idx, ninf, x)
        onehot = onehot + jnp.where(lane == idx, 1.0, 0.0)
        picks.append(idx)
        ti = jnp.where(olane == j, idx, ti)
        tw = jnp.where(olane == j, sj, tw)
        tot = tot + sj
    ti_ref[...] = ti
    tw_ref[...] = tw / tot * ROUTED_SCALE

    @pl.when(pl.program_id(0) == 0)
    def _():
        cnt_scr[...] = jnp.zeros_like(cnt_scr)
    below = (lax.broadcasted_iota(jnp.int32, (tm, tm), 0) > lax.broadcasted_iota(jnp.int32, (tm, tm), 1))
    ltri = jnp.where(below, 1.0, 0.0).astype(BF16)
    prefix = jnp.dot(ltri, onehot.astype(BF16), preferred_element_type=F32) + cnt_scr[...]
    rk = jnp.zeros((tm, LANE), F32)
    for j in range(MOE_TOPK):
        rj = jnp.sum(jnp.where(lane == picks[j], prefix, 0.0), axis=1, keepdims=True)
        rk = jnp.where(olane == j, rj, rk)
    rk_ref[...] = rk.astype(jnp.int32)
    cnt_scr[...] = cnt_scr[...] + jnp.sum(onehot, axis=0, keepdims=True)
    cnt_ref[...] = cnt_scr[...]


def _route(h, w_router, b_router):
    m, d = h.shape
    n_exp = w_router.shape[1]
    tm = _pick(m, (256, 128, 64, 32, 16, 8))
    return pl.pallas_call(
        functools.partial(_router_kernel, n_exp=n_exp),
        grid=(m // tm,),
        in_specs=[pl.BlockSpec((tm, d), lambda i: (i, 0)),
                  pl.BlockSpec((d, n_exp), lambda i: (0, 0)),
                  pl.BlockSpec((1, n_exp), lambda i: (0, 0))],
        out_specs=[pl.BlockSpec((tm, LANE), lambda i: (i, 0))] * 3 + [pl.BlockSpec((1, n_exp), lambda i: (0, 0))],
        out_shape=[jax.ShapeDtypeStruct((m, LANE), jnp.int32), jax.ShapeDtypeStruct((m, LANE), F32),
                   jax.ShapeDtypeStruct((m, LANE), jnp.int32), jax.ShapeDtypeStruct((1, n_exp), F32)],
        scratch_shapes=[pltpu.VMEM((1, n_exp), F32)],
        compiler_params=_cparams(("arbitrary",), 32),
    )(h, w_router, b_router.reshape(1, n_exp))


def _dest_kernel(ti_ref, rk_ref, base_ref, o_ref, *, n_exp):
    tm = ti_ref.shape[0]
    lane = lax.broadcasted_iota(jnp.int32, (tm, n_exp), 1)
    olane = lax.broadcasted_iota(jnp.int32, (tm, LANE), 1)
    out = jnp.zeros((tm, LANE), jnp.int32)
    for j in range(MOE_TOPK):
        base = jnp.sum(jnp.where(lane == ti_ref[:, j:j + 1], base_ref[...], 0), axis=1, keepdims=True)
        out = jnp.where(olane == j, base + rk_ref[:, j:j + 1], out)
    o_ref[...] = out


def _dest_rows(ti, rk, base):
    m = ti.shape[0]
    n_exp = base.shape[1]
    tm = _pick(m, (256, 128, 64, 32, 16, 8))
    tile = pl.BlockSpec((tm, LANE), lambda i: (i, 0))
    return pl.pallas_call(
        functools.partial(_dest_kernel, n_exp=n_exp),
        grid=(m // tm,),
        in_specs=[tile, tile, pl.BlockSpec((1, n_exp), lambda i: (0, 0))],
        out_specs=tile,
        out_shape=jax.ShapeDtypeStruct((m, LANE), jnp.int32),
        compiler_params=_cparams(("arbitrary",), 32),
    )(ti, rk, base)


def _expert_kernel(be_ref, nu_ref, nv_ref, tok_hbm, h_hbm, wg_ref, wu_ref, wd_ref, y_ref,
                   x_buf, idx_smem, row_sem, idx_sem, *, bm):
    i, hf = pl.program_id(0), pl.program_id(1)
    n_used = nu_ref[0]

    def idx_copy(blk):
        return pltpu.make_async_copy(tok_hbm.at[blk], idx_smem, idx_sem)

    def row_copy(blk_slot, r, tok):
        return pltpu.make_async_copy(h_hbm.at[pl.ds(tok, 1)], x_buf.at[blk_slot, pl.ds(r, 1)],
                                     row_sem.at[blk_slot])

    def row_groups(blk):
        return (nv_ref[blk] + DMA_UNROLL - 1) // DMA_UNROLL

    def start_gather(blk):
        slot = blk % 2

        def body(g, _):
            for k in range(DMA_UNROLL):
                r = g * DMA_UNROLL + k
                row_copy(slot, r, idx_smem[r]).start()
            return 0
        lax.fori_loop(0, row_groups(blk), body, 0)

    def wait_gather(blk):
        slot = blk % 2

        def body(g, _):
            for k in range(DMA_UNROLL):
                row_copy(slot, g * DMA_UNROLL + k, 0).wait()
            return 0
        lax.fori_loop(0, row_groups(blk), body, 0)

    @pl.when(jnp.logical_and(i == 0, hf == 0))
    def _():
        x_buf[...] = jnp.zeros_like(x_buf)
        idx_copy(0).start()
        idx_copy(0).wait()
        start_gather(0)

    @pl.when(jnp.logical_and(i < n_used, hf == 0))
    def _():
        wait_gather(i)

    @pl.when(jnp.logical_and(i + 1 < n_used, hf == 0))
    def _():
        idx_copy(i + 1).start()

    @pl.when(jnp.logical_and(i + 1 < n_used, hf == 1))
    def _():
        idx_copy(i + 1).wait()
        start_gather(i + 1)

    def ffn_half():
        x = x_buf[i % 2].astype(BF16)
        g = jnp.dot(x, wg_ref[0].astype(BF16), preferred_element_type=F32)
        u = jnp.dot(x, wu_ref[0].astype(BF16), preferred_element_type=F32)
        a = (_silu(g) * u).astype(BF16)
        return jnp.dot(a, wd_ref[0].astype(BF16), preferred_element_type=F32)

    @pl.when(jnp.logical_and(i < n_used, hf == 0))
    def _():
        y_ref[...] = ffn_half()

    @pl.when(jnp.logical_and(i < n_used, hf == 1))
    def _():
        y_ref[...] = y_ref[...] + ffn_half()


def _experts(h, block_e, n_used, n_valid, row_tok, w_gate, w_up, w_down, bm):
    m, d = h.shape
    n_exp, _, de = w_gate.shape
    nb = block_e.shape[0]
    deh = de // 2
    last_used = lambda i, nu: jnp.minimum(i, nu[0] - 1)
    return pl.pallas_call(
        functools.partial(_expert_kernel, bm=bm),
        grid_spec=pltpu.PrefetchScalarGridSpec(
            num_scalar_prefetch=3,
            grid=(nb, 2),
            in_specs=[pl.BlockSpec(memory_space=pl.ANY),
                      pl.BlockSpec(memory_space=pl.ANY),
                      pl.BlockSpec((1, d, deh), lambda i, hf, be, nu, nv: (be[i], 0, hf)),
                      pl.BlockSpec((1, d, deh), lambda i, hf, be, nu, nv: (be[i], 0, hf)),
                      pl.BlockSpec((1, deh, d), lambda i, hf, be, nu, nv: (be[i], hf, 0))],
            out_specs=pl.BlockSpec((bm, d), lambda i, hf, be, nu, nv: (last_used(i, nu), 0)),
            scratch_shapes=[pltpu.VMEM((2, bm, d), F32), pltpu.SMEM((bm,), jnp.int32),
                            pltpu.SemaphoreType.DMA((2,)), pltpu.SemaphoreType.DMA(())]),
        out_shape=jax.ShapeDtypeStruct((nb * bm, d), F32),
        compiler_params=_cparams(("arbitrary", "arbitrary"), 56),
    )(block_e, n_used, n_valid, row_tok, h, w_gate, w_up, w_down)


def _dispatch(ti, rk, counts, bm):
    m = ti.shape[0]
    n_exp = counts.shape[1]
    n_assign = m * MOE_TOPK
    nb = -(-n_assign // bm) + n_exp
    nblk = (counts[0].astype(jnp.int32) + bm - 1) // bm
    blk_end = jnp.cumsum(nblk)
    blk_start = blk_end - nblk
    n_used = blk_end[-1]
    dest = _dest_rows(ti, rk, (blk_start * bm)[None, :])[:, :MOE_TOPK]
    tok = jnp.arange(n_assign, dtype=jnp.int32) // MOE_TOPK
    row_tok = jnp.zeros((nb * bm,), jnp.int32).at[dest.reshape(-1)].set(tok, unique_indices=True)
    blk = jnp.arange(nb, dtype=jnp.int32)
    be = jnp.minimum(jnp.searchsorted(blk_end, blk, side='right'), n_exp - 1).astype(jnp.int32)
    n_valid = jnp.clip(counts[0].astype(jnp.int32)[be] - (blk - blk_start[be]) * bm, 0, bm)
    n_valid = jnp.where(blk < n_used, n_valid, 0).astype(jnp.int32)
    be = jnp.where(blk < n_used, be, be[jnp.maximum(n_used - 1, 0)])
    return be, n_used.reshape(1).astype(jnp.int32), n_valid, row_tok.reshape(nb, bm), dest


def _glu_kernel(h_ref, wg_ref, wu_ref, o_ref):
    x = h_ref[...].astype(BF16)
    g = jnp.dot(x, wg_ref[...].astype(BF16), preferred_element_type=F32)
    u = jnp.dot(x, wu_ref[...].astype(BF16), preferred_element_type=F32)
    o_ref[...] = (_silu(g) * u).astype(o_ref.dtype)


def _glu(h, wg, wu):
    m, d = h.shape
    de = wg.shape[1]
    bm = _pick(m, (528, 512, 256, 128, 64, 32, 16, 8))
    bn = _pick(de, (256, 128))
    return pl.pallas_call(
        _glu_kernel,
        grid=(m // bm, de // bn),
        in_specs=[pl.BlockSpec((bm, d), lambda i, j: (i, 0)),
                  pl.BlockSpec((d, bn), lambda i, j: (0, j)),
                  pl.BlockSpec((d, bn), lambda i, j: (0, j))],
        out_specs=pl.BlockSpec((bm, bn), lambda i, j: (i, j)),
        out_shape=jax.ShapeDtypeStruct((m, de), BF16),
        compiler_params=_cparams(("arbitrary", "arbitrary"), 48),
    )(h, wg, wu)


def _combine_kernel(pos_hbm, y_hbm, tw_ref, sh_ref, x_ref, gate_ref, gam_ref, bet_ref, o_ref,
                    buf, idx_smem, row_sem, idx_sem, *, tt):
    b, j = pl.program_id(0), pl.program_id(1)
    step = b * pl.num_programs(1) + j
    n = tt * MOE_TOPK
    cp = pltpu.make_async_copy(pos_hbm.at[step], idx_smem, idx_sem)
    cp.start()
    cp.wait()

    def row_copy(a, src):
        return pltpu.make_async_copy(y_hbm.at[pl.ds(src, 1)], buf.at[a % MOE_TOPK, pl.ds(a // MOE_TOPK, 1)],
                                     row_sem)

    def start(a, _):
        row_copy(a, idx_smem[a]).start()
        return 0
    lax.fori_loop(0, n, start, 0, unroll=DMA_UNROLL)

    for e in range(MOE_TOPK):
        pltpu.make_async_copy(y_hbm.at[pl.ds(0, tt)], buf.at[e], row_sem).wait()

    tw = tw_ref[0]
    routed = buf[0] * tw[:, 0:1]
    for e in range(1, MOE_TOPK):
        routed = routed + buf[e] * tw[:, e:e + 1]
    z = DN_ALPHA * x_ref[0] + gate_ref[0] * (routed + sh_ref[0])
    o_ref[0] = _layer_norm(z) * gam_ref[...] + bet_ref[...]


def _combine(pos, y_sorted, top_w, shared, x1, gate, gamma, beta):
    b, t, d = x1.shape
    tt = _pick(t, (64, 32, 16, 8))
    n = tt * MOE_TOPK
    pos_steps = pos.reshape(-1, n)
    row = lambda w: pl.BlockSpec((1, tt, w), lambda i, j: (i, j, 0))
    mod = pl.BlockSpec((1, 1, d), lambda i, j: (i, 0, 0))
    vec = pl.BlockSpec((1, d), lambda i, j: (0, 0))
    return pl.pallas_call(
        functools.partial(_combine_kernel, tt=tt),
        grid=(b, t // tt),
        in_specs=[pl.BlockSpec(memory_space=pl.ANY), pl.BlockSpec(memory_space=pl.ANY),
                  row(MOE_TOPK), row(d), row(d), mod, vec, vec],
        out_specs=row(d),
        out_shape=jax.ShapeDtypeStruct((b, t, d), F32),
        scratch_shapes=[pltpu.VMEM((MOE_TOPK, tt, d), F32), pltpu.SMEM((n,), jnp.int32),
                        pltpu.SemaphoreType.DMA(()), pltpu.SemaphoreType.DMA(())],
        compiler_params=_cparams(("arbitrary", "arbitrary"), 40),
    )(pos_steps, y_sorted, top_w, shared, x1, gate, gamma.reshape(1, d), beta.reshape(1, d))


def kernel(x_prompt, x_sample, c_prompt, c_sample, cache_k, cache_v, cache_idx_k, state_ret, page_table, w_cond, b_cond, w_in, w_ret_o, w_att_o, w_out, ln_mix_g, ln_mix_b, w_router, b_router, w_exp_gate, w_exp_up, w_exp_down, w_sh_gate, w_sh_up, w_sh_down, ln_ffn_g, ln_ffn_b):
    bp, tp, d = x_prompt.shape
    bs, ts, _ = x_sample.shape
    assert bp == 1, "prompt group kernels are written for a single prompt sequence"
    n_pool, page, ah, dh = cache_k.shape
    idh = cache_idx_k.shape[2]
    rh, dk, dv = state_ret.shape[1:]
    n_pages = page_table.shape[1]
    past = n_pages * page
    n_exp = w_router.shape[1]
    aw, rqk, rv_w = ah * dh, rh * dk, rh * dv
    d_in = w_in.shape[1]
    ih = d_in - (2 * rqk + 2 * rv_w + 3 * aw + idh + 2 * d)
    ih = ih // (idh + 1)
    iq_w = ih * idh
    widths = (rqk, rqk, rv_w, rv_w, aw, aw, aw, iq_w, idh, ih, d, d)
    assert sum(widths) == d_in
    offs = np.concatenate([[0], np.cumsum(widths)]).tolist()
    o_rq, o_rk, o_rv, o_rg, o_aq, o_ak, o_av, o_iq, o_ik, o_iw, o_ga, o_gb = offs[:12]
    mp, ms = bp * tp, bs * ts
    m = mp + ms

    mod = _modulation(jnp.concatenate([c_prompt, c_sample], axis=0), w_cond, b_cond)
    mods_p = [v[:, None, :] for v in jnp.split(mod[:bp], 6, axis=-1)]
    mods_s = [v[:, None, :] for v in jnp.split(mod[bp:], 6, axis=-1)]

    h = jnp.concatenate([_ln_mod(x_prompt, mods_p[1], mods_p[0], BF16).reshape(mp, d),
                         _ln_mod(x_sample, mods_s[1], mods_s[0], BF16).reshape(ms, d)], axis=0)

    pos = jnp.concatenate([jnp.tile(jnp.arange(tp, dtype=jnp.int32), bp),
                           jnp.tile(past + jnp.arange(ts, dtype=jnp.int32), bs)]).astype(F32)
    half = dk // 2
    inv = ROT_BASE ** (-jnp.arange(half, dtype=F32) / half)
    ang = pos[:, None] * inv[None, :]
    cos, sin = jnp.cos(ang), jnp.sin(ang)

    rq, = _proj(h, w_in, o_rq, rqk, [BF16], _ep_rotate(dk, None), extras=(cos, sin))
    rk, = _proj(h, w_in, o_rk, rqk, [BF16], _ep_rotate(dk, dk ** -0.5), extras=(cos, sin))
    rv, = _proj(h, w_in, o_rv, rv_w, [BF16], _ep_store())
    rg, = _proj(h, w_in, o_rg, rv_w, [F32], _ep_store())
    aq, = _proj(h, w_in, o_aq, aw, [BF16], _ep_store())
    ak, ak_b = _proj(h, w_in, o_ak, aw, [F32, BF16], _ep_store())
    av, av_b = _proj(h, w_in, o_av, aw, [F32, BF16], _ep_store())
    iq, = _proj(h, w_in, o_iq, iq_w, [BF16], _ep_store(idh ** -0.5))
    ik, ik_b = _proj(h, w_in, o_ik, idh, [F32, BF16], _ep_store(), bn=LANE)
    iw, = _proj(h, w_in, o_iw, LANE, [F32], _ep_store(ih ** -0.5), bn=LANE)
    w_gates = w_in[:, o_ga:]
    ga, = _proj(h, w_gates, 0, d, [F32], _ep_store())
    gb, = _proj(h, w_gates, d, d, [F32], _ep_store())

    chunk = _pick(tp, (256, 128))
    ra_p, s_p = _retention(rq, rk, rv, rg, jnp.zeros((bp, rh, dk, dv), F32), bp, tp, chunk, chunk, rh, dk, dv)
    pad_s = lambda a: jnp.pad(a[mp:].reshape(bs, ts, -1), ((0, 0), (0, LANE - ts), (0, 0))).reshape(bs * LANE, -1)
    ra_s, s_s = _retention(pad_s(rq), pad_s(rk), pad_s(rv), pad_s(rg), state_ret.astype(F32),
                           bs, LANE, LANE, ts, rh, dk, dv)
    ra_s = ra_s.reshape(bs, LANE, rv_w)[:, :ts]

    tq = _pick(tp, (256, 128))
    tk_sel = _pick(tp, (2048, 1024, 512, 256, 128))
    tk_att = _pick(tk_sel, (1024, 512, 256, 128))
    bias_p = _select_prompt(iq, ik_b[:mp], iw, tp, ih, idh, min(TOPK_MAX, tp // 4), tq, tk_sel)
    att_p = _attention_prompt(aq, ak_b, av_b, bias_p, tp, ah, dh, tq, tk_att)

    pp = _pick(n_pages, (4, 2, 1))
    hm = lambda a, nh: a.reshape(bs, ts, nh, -1).transpose(0, 2, 1, 3).reshape(bs, nh * ts, -1)
    iq_rows = hm(iq[mp:], ih)
    iw_rows = iw[mp:, :ih].reshape(bs, ts, ih).transpose(0, 2, 1).reshape(bs, ih * ts, 1)
    pad_new = lambda a: jnp.pad(a[mp:].reshape(bs, ts, -1), ((0, 0), (0, LANE - ts), (0, 0)))
    bias_past, bias_new = _select_sample(page_table, cache_idx_k, iq_rows, iw_rows, pad_new(ik_b),
                                         ts, ih, min(TOPK_MAX, (past + ts) // 4), _pick(n_pages, (16, 8, 4, 2, 1)))
    flat_kv = lambda c: c.reshape(n_pool * page * ah, dh)
    new_kv = lambda a: jnp.pad(a[mp:].reshape(bs, ts * ah, dh), ((0, 0), (0, (page - ts) * ah), (0, 0)))
    att_s = _attention_sample(page_table, flat_kv(cache_k), flat_kv(cache_v), hm(aq[mp:], ah).astype(F32),
                              bias_past, bias_new, new_kv(ak), new_kv(av), ts, ah, dh, pp)
    att_s = att_s.reshape(bs, ah, ts, dh).transpose(0, 2, 1, 3)

    ra = jnp.concatenate([ra_p, ra_s.reshape(ms, rv_w)], axis=0)
    att = jnp.concatenate([att_p, att_s.reshape(ms, aw)], axis=0)
    u = _merge(ra, w_ret_o, att, w_att_o, ga, gb)
    mix, = _proj(u, w_out, 0, d, [F32], _ep_store())
    x1_p, h2_p = _post_ln(x_prompt, mix.reshape(1, m, d), mods_p[2], ln_mix_g, ln_mix_b, mods_p[4], mods_p[3])
    x1_s, h2_s = _post_ln(x_sample, mix[mp:].reshape(bs, ts, d), mods_s[2], ln_mix_g, ln_mix_b, mods_s[4], mods_s[3])
    h2 = jnp.concatenate([h2_p.reshape(mp, d), h2_s.reshape(ms, d)], axis=0)

    top_i, top_w, rank, counts = _route(h2, w_router, b_router)
    top_w = top_w[:, :MOE_TOPK]
    n_assign = m * MOE_TOPK
    bm = 320 if n_assign // n_exp >= 256 else _pick(max(n_assign // n_exp, 8), (128, 64, 32, 16, 8))
    block_e, n_used, n_valid, row_tok, dest = _dispatch(top_i, rank, counts, bm)
    y_sorted = _experts(h2, block_e, n_used, n_valid, row_tok, w_exp_gate, w_exp_up, w_exp_down, bm)
    shared, = _proj(_glu(h2, w_sh_gate, w_sh_up), w_sh_down, 0, d, [F32], _ep_store())

    y_p = _combine(dest[:mp], y_sorted, top_w[:mp].reshape(bp, tp, MOE_TOPK), shared.reshape(1, m, d),
                   x1_p, mods_p[5], ln_ffn_g, ln_ffn_b)
    y_s = _combine(dest[mp:], y_sorted, top_w[mp:].reshape(bs, ts, MOE_TOPK), shared[mp:].reshape(bs, ts, d),
                   x1_s, mods_s[5], ln_ffn_g, ln_ffn_b)

    return (y_p, y_s,
            ak[:mp].reshape(bp, tp, ah, dh), av[:mp].reshape(bp, tp, ah, dh), ik[:mp].reshape(bp, tp, idh),
            s_p.reshape(bp, rh, dk, dv),
            ak[mp:].reshape(bs, ts, ah, dh), av[mp:].reshape(bs, ts, ah, dh), ik[mp:].reshape(bs, ts, idh),
            s_s)
```
